```python
import math
import jax, jax.numpy as jnp
from jax import lax
import numpy as np

D_MODEL = 1024
BATCH = 32
SEQ = 2048
DEPTH = 1
DEC_BATCH = 128
DEC_SEQ = 8
PAST_LEN = 16384
PAGE_SIZE = 128

N_META = 16
D_MIX = D_MODEL
MLA_HEADS = 8
MLA_NOPE = 64
MLA_ROPE = 32
MLA_V = 64
Q_RANK = 256
KV_RANK = 128
FOX_HEADS = 8
FOX_DIM = 64
D_MLA = MLA_HEADS * MLA_V
D_FOX = FOX_HEADS * FOX_DIM
Q_BLOCK = 128
ROPE_THETA = 10000.0
EPS = 1e-6
NEG = -1e30
FORGET_BIAS_INIT = 3.0
MLA_SCALE = (MLA_NOPE + MLA_ROPE) ** -0.5
FOX_SCALE = FOX_DIM ** -0.5
SPLITS = (Q_RANK, KV_RANK, MLA_ROPE, D_MLA, D_FOX, D_FOX, D_FOX, FOX_HEADS, D_FOX)
D_IN = Q_RANK + KV_RANK + MLA_ROPE + D_MLA + 4 * D_FOX + FOX_HEADS

kernel_name = 'hymba_mla_fox_sandwich_step'


def _rms(x, g):
    xf = x.astype(jnp.float32)
    y = xf * lax.rsqrt(jnp.mean(xf * xf, axis=-1, keepdims=True) + EPS)
    return (y * g.astype(jnp.float32)).astype(x.dtype)


def _rope_tables(pos):
    half = MLA_ROPE // 2
    inv = ROPE_THETA ** (-jnp.arange(half, dtype=jnp.float32) / half)
    ang = pos.astype(jnp.float32)[:, None] * inv[None, :]
    return jnp.cos(ang), jnp.sin(ang)


def _rope(x, cos, sin):
    half = MLA_ROPE // 2
    xf = x.astype(jnp.float32)
    x1, x2 = xf[..., :half], xf[..., half:]
    return jnp.concatenate([x1 * cos - x2 * sin, x2 * cos + x1 * sin], axis=-1).astype(x.dtype)


def _split_proj(p):
    outs, o = [], 0
    for w in SPLITS:
        outs.append(p[..., o:o + w])
        o += w
    return outs


def _project(h, pos, w_in, g_q, w_uq, g_kv, w_uk, b_f):
    B, T, _ = h.shape
    cq, ckv, kpe, z_m, fq, fk, fv, f_logit, z_f = _split_proj(h @ w_in)
    cos, sin = _rope_tables(pos)
    q = (_rms(cq, g_q) @ w_uq).reshape(B, T, MLA_HEADS, MLA_NOPE + MLA_ROPE)
    q_nope, q_pe = q[..., :MLA_NOPE], q[..., MLA_NOPE:]
    q_pe = _rope(q_pe, cos[:, None, :], sin[:, None, :])
    q_lat = jnp.einsum('bthd,rhd->bthr', q_nope, w_uk)
    ckv = _rms(ckv, g_kv)
    kpe = _rope(kpe, cos, sin)
    fq = fq.reshape(B, T, FOX_HEADS, FOX_DIM)
    fk = fk.reshape(B, T, FOX_HEADS, FOX_DIM)
    fv = fv.reshape(B, T, FOX_HEADS, FOX_DIM)
    lf = jax.nn.log_sigmoid((f_logit + b_f).astype(jnp.float32))
    return q_lat, q_pe, ckv, kpe, fq, fk, fv, lf, z_m, z_f


def _mla_scores(q_lat, q_pe, ckv, kpe):
    s = jnp.einsum('bthr,bsr->bhts', q_lat, ckv, preferred_element_type=jnp.float32)
    s = s + jnp.einsum('bthe,bse->bhts', q_pe, kpe, preferred_element_type=jnp.float32)
    return s * MLA_SCALE


def _fox_scores(fq, fk):
    return jnp.einsum('bthd,bshd->bhts', fq, fk, preferred_element_type=jnp.float32) * FOX_SCALE


def _attend_block(q_lat, q_pe, fq, Fq, qpos, ckv, kpe, fk, fv, Fk, kpos):
    mask = (kpos[None, :] <= qpos[:, None])[None, None]
    p_m = jax.nn.softmax(jnp.where(mask, _mla_scores(q_lat, q_pe, ckv, kpe), NEG), axis=-1)
    o_lat = jnp.einsum('bhts,bsr->bthr', p_m.astype(ckv.dtype), ckv,
                       preferred_element_type=jnp.float32).astype(ckv.dtype)
    bias = jnp.swapaxes(Fq, 1, 2)[..., :, None] - jnp.swapaxes(Fk, 1, 2)[..., None, :]
    p_f = jax.nn.softmax(jnp.where(mask, _fox_scores(fq, fk) + bias, NEG), axis=-1)
    o_f = jnp.einsum('bhts,bshd->bthd', p_f.astype(fv.dtype), fv,
                     preferred_element_type=jnp.float32).astype(fv.dtype)
    return o_lat, o_f


def _online(state, s, v, spec):
    m, l, acc = state
    m_new = jnp.maximum(m, jnp.max(s, axis=-1))
    corr = jnp.exp(m - m_new)
    p = jnp.exp(s - m_new[..., None])
    l = l * corr + jnp.sum(p, axis=-1)
    acc = acc * corr[..., None] + jnp.einsum(spec, p.astype(v.dtype), v, preferred_element_type=jnp.float32)
    return (m_new, l, acc)


def _attend_paged(q_lat, q_pe, fq, lf, ckv, kpe, fk, fv,
                  c_ckv, c_kpe, c_fk, c_fv, c_lf, page_table, l):
    f32 = jnp.float32
    Bd, T = q_lat.shape[:2]
    G = jnp.cumsum(lf, axis=1)
    Gq = jnp.swapaxes(G, 1, 2)[..., None]
    lf_past = c_lf[l, page_table].astype(f32)
    flat = lf_past.reshape(Bd, -1, FOX_HEADS)
    suffix = (lax.cumsum(flat, axis=1, reverse=True) - flat).reshape(lf_past.shape)
    suffix = jnp.moveaxis(suffix, 1, 0)

    def init(h, dv):
        return (jnp.full((Bd, h, T), NEG, f32), jnp.zeros((Bd, h, T), f32), jnp.zeros((Bd, h, T, dv), f32))

    def body(carry, inp):
        st_m, st_f = carry
        pid, suf = inp
        p_ckv, p_kpe = c_ckv[l, pid], c_kpe[l, pid]
        p_fk, p_fv = c_fk[l, pid], c_fv[l, pid]
        st_m = _online(st_m, _mla_scores(q_lat, q_pe, p_ckv, p_kpe), p_ckv, 'bhts,bsr->bhtr')
        s_f = _fox_scores(fq, p_fk) + Gq + jnp.swapaxes(suf, 1, 2)[:, :, None, :]
        st_f = _online(st_f, s_f, p_fv, 'bhts,bshd->bhtd')
        return (st_m, st_f), None

    (st_m, st_f), _ = lax.scan(body, (init(MLA_HEADS, KV_RANK), init(FOX_HEADS, FOX_DIM)),
                               (page_table.T, suffix))
    mask = (jnp.arange(T)[None, :] <= jnp.arange(T)[:, None])[None, None]
    st_m = _online(st_m, jnp.where(mask, _mla_scores(q_lat, q_pe, ckv, kpe), NEG), ckv, 'bhts,bsr->bhtr')
    bias = Gq - jnp.swapaxes(G, 1, 2)[:, :, None, :]
    st_f = _online(st_f, jnp.where(mask, _fox_scores(fq, fk) + bias, NEG), fv, 'bhts,bshd->bhtd')
    o_lat = jnp.transpose(st_m[2] / st_m[1][..., None], (0, 2, 1, 3)).astype(ckv.dtype)
    o_f = jnp.transpose(st_f[2] / st_f[1][..., None], (0, 2, 1, 3)).astype(fv.dtype)
    return o_lat, o_f


def _merge(o_lat, o_f, z_m, z_f, w_uv, g_om, g_of, w_o, g_post):
    B, T = o_lat.shape[:2]
    o_m = jnp.einsum('bthr,rhd->bthd', o_lat, w_uv).reshape(B, T, D_MLA)
    o_f = o_f.reshape(B, T, D_FOX)
    y = jnp.concatenate([_rms(o_m, g_om) * jax.nn.silu(z_m), _rms(o_f, g_of) * jax.nn.silu(z_f)], axis=-1)
    return _rms(y @ w_o, g_post)


def setup_inputs(seed: int = 0) -> dict:
    key = jax.random.key(seed)
    ks = jax.random.split(key, 24)
    n_pages = PAST_LEN // PAGE_SIZE
    n_used = DEC_BATCH * n_pages
    n_pool = n_used + n_used // 4

    def nrm(k, shape, scale=1.0):
        return scale * jax.random.normal(k, shape, jnp.float32)

    def gain(k, n):
        return 1.0 + 0.05 * nrm(k, (DEPTH, n))

    return {
        'x_prompt': nrm(ks[0], (BATCH, SEQ, D_MODEL)),
        'x_sample': nrm(ks[1], (DEC_BATCH, DEC_SEQ, D_MODEL)),
        'cache_ckv': nrm(ks[2], (DEPTH, n_pool, PAGE_SIZE, KV_RANK)),
        'cache_kpe': nrm(ks[3], (DEPTH, n_pool, PAGE_SIZE, MLA_ROPE)),
        'cache_fox_k': nrm(ks[4], (DEPTH, n_pool, PAGE_SIZE, FOX_HEADS, FOX_DIM)),
        'cache_fox_v': nrm(ks[5], (DEPTH, n_pool, PAGE_SIZE, FOX_HEADS, FOX_DIM)),
        'cache_fox_logf': jax.nn.log_sigmoid(FORGET_BIAS_INIT + nrm(ks[6], (DEPTH, n_pool, PAGE_SIZE, FOX_HEADS))),
        'page_table': jax.random.permutation(ks[7], n_pool)[:n_used].reshape(DEC_BATCH, n_pages).astype(jnp.int32),
        'meta_tokens': nrm(ks[8], (N_META, D_MODEL)),
        'g_pre': gain(ks[9], D_MODEL),
        'g_post': gain(ks[10], D_MODEL),
        'w_in': nrm(ks[11], (DEPTH, D_MODEL, D_IN), D_MODEL ** -0.5),
        'g_q': gain(ks[12], Q_RANK),
        'w_uq': nrm(ks[13], (DEPTH, Q_RANK, MLA_HEADS * (MLA_NOPE + MLA_ROPE)), Q_RANK ** -0.5),
        'g_kv': gain(ks[14], KV_RANK),
        'w_uk': nrm(ks[15], (DEPTH, KV_RANK, MLA_HEADS, MLA_NOPE), KV_RANK ** -0.5),
        'w_uv': nrm(ks[16], (DEPTH, KV_RANK, MLA_HEADS, MLA_V), KV_RANK ** -0.5),
        'b_f': FORGET_BIAS_INIT + 0.5 * nrm(ks[17], (DEPTH, FOX_HEADS)),
        'g_out_mla': gain(ks[18], D_MLA),
        'g_out_fox': gain(ks[19], D_FOX),
        'w_o': nrm(ks[20], (DEPTH, D_MIX, D_MODEL), D_MIX ** -0.5),
    }


def reference(x_prompt, x_sample, cache_ckv, cache_kpe, cache_fox_k, cache_fox_v, cache_fox_logf,
              page_table, meta_tokens, g_pre, g_post, w_in, g_q, w_uq, g_kv, w_uk, w_uv, b_f,
              g_out_mla, g_out_fox, w_o):
    B, S, _ = x_prompt.shape
    T = x_sample.shape[1]
    past_len = page_table.shape[1] * cache_ckv.shape[2]
    L = S + N_META
    nb = S // Q_BLOCK
    x = jnp.concatenate([jnp.broadcast_to(meta_tokens.astype(x_prompt.dtype)[None], (B, N_META, D_MODEL)),
                         x_prompt], axis=1)
    xs = x_sample
    pos_p = jnp.arange(L, dtype=jnp.int32)
    pos_s = past_len + jnp.arange(T, dtype=jnp.int32)
    ckv_p, kpe_p, fk_p, fv_p, flf_p = [], [], [], [], []
    ckv_s, kpe_s, fk_s, fv_s, flf_s = [], [], [], [], []

    def to_blocks(a):
        a = a[:, N_META:]
        return jnp.moveaxis(a.reshape((B, nb, Q_BLOCK) + a.shape[2:]), 1, 0)

    def from_blocks(a):
        return jnp.moveaxis(a, 0, 1).reshape((B, S) + a.shape[3:])

    for l in range(DEPTH):
        lw = (w_in[l], g_q[l], w_uq[l], g_kv[l], w_uk[l], b_f[l])
        q_lat, q_pe, ckv, kpe, fq, fk, fv, lf, z_m, z_f = _project(_rms(x, g_pre[l]), pos_p, *lw)
        F = jnp.cumsum(lf, axis=1)
        m = N_META
        o_lat0, o_f0 = _attend_block(q_lat[:, :m], q_pe[:, :m], fq[:, :m], F[:, :m], pos_p[:m],
                                     ckv[:, :m], kpe[:, :m], fk[:, :m], fv[:, :m], F[:, :m], pos_p[:m])
        blk_in = (to_blocks(q_lat), to_blocks(q_pe), to_blocks(fq), to_blocks(F),
                  pos_p[N_META:].reshape(nb, Q_BLOCK))
        o_lat_b, o_f_b = lax.map(lambda a: _attend_block(*a, ckv, kpe, fk, fv, F, pos_p), blk_in)
        o_lat = jnp.concatenate([o_lat0, from_blocks(o_lat_b)], axis=1)
        o_f = jnp.concatenate([o_f0, from_blocks(o_f_b)], axis=1)
        x = x + _merge(o_lat, o_f, z_m, z_f, w_uv[l], g_out_mla[l], g_out_fox[l], w_o[l], g_post[l])
        ckv_p.append(ckv); kpe_p.append(kpe); fk_p.append(fk); fv_p.append(fv); flf_p.append(lf.astype(ckv.dtype))
        q_lat, q_pe, ckv, kpe, fq, fk, fv, lf, z_m, z_f = _project(_rms(xs, g_pre[l]), pos_s, *lw)
        o_lat, o_f = _attend_paged(q_lat, q_pe, fq, lf, ckv, kpe, fk, fv, cache_ckv, cache_kpe,
                                   cache_fox_k, cache_fox_v, cache_fox_logf, page_table, l)
        xs = xs + _merge(o_lat, o_f, z_m, z_f, w_uv[l], g_out_mla[l], g_out_fox[l], w_o[l], g_post[l])
        ckv_s.append(ckv); kpe_s.append(kpe); fk_s.append(fk); fv_s.append(fv); flf_s.append(lf.astype(ckv.dtype))

    y_prompt = x[:, N_META:]
    y_sample = xs
    return (y_prompt, y_sample,
            jnp.stack(ckv_p), jnp.stack(kpe_p), jnp.stack(fk_p), jnp.stack(fv_p), jnp.stack(flf_p),
            jnp.stack(ckv_s), jnp.stack(kpe_s), jnp.stack(fk_s), jnp.stack(fv_s), jnp.stack(flf_s))
```

```python
import functools

import jax
import jax.numpy as jnp
from jax import lax
from jax.experimental import pallas as pl
from jax.experimental.pallas import tpu as pltpu

F32 = jnp.float32
BF16 = jnp.bfloat16

D_MODEL = 1024
N_META = 16
HEADS = 8
MLA_NOPE = 64
MLA_ROPE = 32
Q_RANK = 256
KV_RANK = 128
FOX_DIM = 64
D_FOX = HEADS * FOX_DIM
D_MLA = HEADS * 64
ROPE_THETA = 10000.0
EPS = 1e-6
NEG = -1e30
MLA_SCALE = (MLA_NOPE + MLA_ROPE) ** -0.5
FOX_SCALE = FOX_DIM ** -0.5
SPLITS = (Q_RANK, KV_RANK, MLA_ROPE, D_MLA, D_FOX, D_FOX, D_FOX, HEADS, D_FOX)

LANES = 128
C_CQ, C_CKV, C_KPE, C_KPESW, C_ZM, C_FQ, C_ZF, C_FL, C_FK, C_FV, C_END = (
    0, 256, 384, 512, 640, 1152, 1664, 2176, 2304, 2816, 3328)

PROJ_TM = 256
ROWS_TM = 128
ATT_TQ = 256
MERGE_TM = 512
SUF_PAGES = 8
DEC_PAGES = 8
VMEM_LIMIT = 56 * 1024 * 1024


def _nt(a, b):
    return lax.dot_general(a, b, (((1,), (1,)), ((), ())), preferred_element_type=F32)


def _mm(a, b):
    return jnp.dot(a, b, preferred_element_type=F32)


def _rms(x, g):
    return x * lax.rsqrt(jnp.mean(x * x, axis=-1, keepdims=True) + EPS) * g


def _split3(x):
    hi = x.astype(BF16)
    r1 = x - hi.astype(F32)
    mid = r1.astype(BF16)
    lo = (r1 - mid.astype(F32)).astype(BF16)
    return hi, mid, lo


def _mm_exact(x, u):
    hi, mid, lo = _split3(x)
    return _mm(hi, u) + _mm(mid, u) + _mm(lo, u)


def _log_sigmoid(x):
    return jnp.minimum(x, 0.0) - jnp.log(1.0 + jnp.exp(-jnp.abs(x)))


def _silu(x):
    return x / (1.0 + jnp.exp(-x))


def _proj_math(x, cos, sin, g_pre, w_a, g_q, w_uq, w_uk, g_kv, b_f):
    h = _rms(x, g_pre).astype(BF16)
    p = _mm(h, w_a)
    cqn = _rms(p[:, C_CQ:C_CKV], g_q).astype(BF16)
    qall = _mm(cqn, w_uq)
    cos2 = jnp.concatenate([cos, cos], axis=1)
    sin2 = jnp.concatenate([sin, sin], axis=1)
    q_pe = (qall[:, 512:768] * cos2 + qall[:, 768:1024] * sin2) * MLA_SCALE
    q_lat = _mm(qall[:, 0:512].astype(BF16), w_uk) * MLA_SCALE
    ckv = _rms(p[:, C_CKV:C_KPE], g_kv)
    kr4 = p[:, C_KPE:C_KPESW] * cos + p[:, C_KPESW:C_ZM] * sin
    zs = jnp.concatenate([_silu(p[:, C_ZM:C_FQ]), _silu(p[:, C_ZF:C_FL])], axis=1)
    fq = p[:, C_FQ:C_ZF] * FOX_SCALE
    lf = _log_sigmoid(p[:, C_FL:C_FK] + b_f)
    return h, p, dict(q_lat=q_lat, q_pe=q_pe, ckv=ckv, kr4=kr4, zs=zs, fq=fq, lf=lf)


def _proj_rows_kernel(running, x_ref, cos_ref, sin_ref, gpre_ref, win_ref, gq_ref, wuq_ref, wuk_ref,
                      gkv_ref, bf_ref, wt_ref, u_ref,
                      ckv_o, kpet_o, fkt_o, fvt_o, lft_o, ft_o, kc_o, fkt16_o, fvt16_o,
                      qlat_o, qpe_o, fq_o, zs_o, kpe_o, fk16_o, fv16_o, carry_ref):
    h, p, r = _proj_math(x_ref[...], cos_ref[...], sin_ref[...], gpre_ref[...], win_ref[...], gq_ref[...],
                         wuq_ref[...], wuk_ref[...], gkv_ref[...], bf_ref[...])
    pt = _nt(wt_ref[...], h)
    ckv_o[...] = r["ckv"]
    kpet_o[...] = r["kr4"].T[0:MLA_ROPE, :]
    fkt_o[...] = pt[0:D_FOX]
    fvt_o[...] = pt[D_FOX:2 * D_FOX]
    lft = r["lf"].T[0:HEADS, :]
    lft_o[...] = lft
    if running:
        @pl.when(pl.program_id(0) == 0)
        def _():
            carry_ref[...] = jnp.zeros(carry_ref.shape, F32)
        ft = carry_ref[...] + lft
        carry_ref[...] = ft
    else:
        ft = _mm_exact(lft, u_ref[...])
    ft_o[...] = ft
    kc_o[...] = jnp.concatenate([r["ckv"], r["kr4"]], axis=1).astype(BF16)
    fkt16_o[...] = pt[0:D_FOX].astype(BF16)
    fvt16_o[...] = pt[D_FOX:2 * D_FOX].astype(BF16)
    qlat_o[...] = r["q_lat"].astype(BF16)
    qpe_o[...] = r["q_pe"].astype(BF16)
    fq_o[...] = r["fq"].astype(BF16)
    zs_o[...] = r["zs"].astype(BF16)
    kpe_o[...] = r["kr4"][:, 0:MLA_ROPE].astype(BF16)
    fk16_o[...] = p[:, C_FK:C_FV].astype(BF16)
    fv16_o[...] = p[:, C_FV:C_END].astype(BF16)


def _proj_prompt_kernel(x_ref, cos_ref, sin_ref, mckv_ref, mkpet_ref, mfkt_ref, mfvt_ref, mlft_ref, mft_ref,
                        gpre_ref, win_ref, gq_ref, wuq_ref, wuk_ref, gkv_ref, bf_ref, wt_ref, u_ref,
                        ckv_o, kpet_o, fkt_o, fvt_o, lft_o,
                        q_o, kc_o, fq_o, fkt16_o, fvt16_o, zs_o, ft_o,
                        carry_ref, tail_kpe, tail_fk, tail_fv, tail_lf):
    c = pl.program_id(1)
    n_c = pl.num_programs(1)
    tm = x_ref.shape[1]

    @pl.when(c == 0)
    def _():
        ckv_o[0, 0, 0:N_META, :] = mckv_ref[0:N_META, :]
        tail_kpe[...] = mkpet_ref[...]
        tail_fk[...] = mfkt_ref[...]
        tail_fv[...] = mfvt_ref[...]
        tail_lf[...] = mlft_ref[...]
        carry_ref[...] = jnp.broadcast_to(mft_ref[:, N_META - 1:N_META], (HEADS, LANES))

    h, _, r = _proj_math(x_ref[0], cos_ref[...], sin_ref[...], gpre_ref[...], win_ref[:, 0:C_FK],
                         gq_ref[...], wuq_ref[...], wuk_ref[...], gkv_ref[...], bf_ref[...])
    pt = _nt(wt_ref[...], h)
    lft = r["lf"].T[0:HEADS, :]

    off = pl.multiple_of(N_META + c * tm, N_META)
    ckv_o[0, 0, pl.ds(off, tm), :] = r["ckv"]

    first = lax.broadcasted_iota(jnp.int32, (1, LANES), 1) < N_META

    def shifted_store(out_ref, tail_ref, cur):
        prev = tail_ref[...]
        for k in range(tm // LANES):
            rk = pltpu.roll(cur[:, LANES * k:LANES * (k + 1)], N_META, axis=1)
            col = pl.multiple_of(c * tm + LANES * k, LANES)
            out_ref[0, 0, :, pl.ds(col, LANES)] = jnp.where(first, prev, rk)
            prev = rk
        tail_ref[...] = prev

        @pl.when(c == n_c - 1)
        def _():
            end = out_ref.shape[3] - N_META
            out_ref[0, 0, :, end:end + N_META] = prev[:, 0:N_META]

    shifted_store(kpet_o, tail_kpe, r["kr4"].T[0:MLA_ROPE, :])
    shifted_store(fkt_o, tail_fk, pt[0:D_FOX])
    shifted_store(fvt_o, tail_fv, pt[D_FOX:2 * D_FOX])
    shifted_store(lft_o, tail_lf, lft)

    kc_o[0] = jnp.concatenate([r["ckv"], r["kr4"]], axis=1).astype(BF16)
    fkt16_o[0] = pt[0:D_FOX].astype(BF16)
    fvt16_o[0] = pt[D_FOX:2 * D_FOX].astype(BF16)
    fq_o[0] = r["fq"].astype(BF16)
    zs_o[0] = r["zs"].astype(BF16)
    group = lax.broadcasted_iota(jnp.int32, (1, LANES), 1) // MLA_ROPE
    for hh in range(HEADS):
        half = r["q_pe"][:, LANES * (hh // 4):LANES * (hh // 4 + 1)]
        pe = jnp.where(group == hh % 4, half, 0.0)
        q_o[0, hh] = jnp.concatenate([r["q_lat"][:, LANES * hh:LANES * (hh + 1)], pe], axis=1).astype(BF16)
    ft = carry_ref[:, 0:1] + _mm_exact(lft, u_ref[...])
    ft_o[0] = ft
    carry_ref[...] = jnp.broadcast_to(ft[:, tm - 1:tm], (HEADS, LANES))


def _attn_prompt_kernel(q_ref, fq_ref, kc_ref, fkt_ref, fvt_ref, ft_ref, mkc_ref, mfkt_ref, mfvt_ref, mft_ref,
                        olat_o, of_o, mm_s, lm_s, am_s, mf_s, lf_s, af_s):
    tq = fq_ref.shape[1]
    qi = pl.program_id(1)
    q = q_ref[0].reshape(HEADS * tq, 2 * LANES)
    fq = fq_ref[0]
    lo = lax.broadcasted_iota(jnp.int32, (1, LANES), 1) < FOX_DIM
    zero = jnp.zeros((), BF16)
    qp = []
    for p in range(HEADS // 2):
        blk = fq[:, LANES * p:LANES * (p + 1)]
        qp.append(jnp.concatenate([jnp.where(lo, blk, zero), jnp.where(lo, zero, blk)], axis=0))

    mm_s[...] = jnp.full(mm_s.shape, NEG, F32)
    lm_s[...] = jnp.zeros(lm_s.shape, F32)
    am_s[...] = jnp.zeros(am_s.shape, F32)
    mf_s[...] = jnp.full(mf_s.shape, NEG, F32)
    lf_s[...] = jnp.zeros(lf_s.shape, F32)
    af_s[...] = jnp.zeros(af_s.shape, F32)

    def online(s, pv, m_ref, l_ref, a_ref, idx):
        m_old = m_ref[idx]
        m_new = jnp.maximum(m_old, jnp.max(s, axis=1, keepdims=True))
        alpha = jnp.exp(m_old - m_new)
        p = jnp.exp(s - m_new)
        l_ref[idx] = alpha * l_ref[idx] + jnp.sum(p, axis=1, keepdims=True)
        a_ref[idx] = alpha * a_ref[idx] + pv(p.astype(BF16))
        m_ref[idx] = m_new

    def update(kc, fkt, fvt, ft, mask):
        tk = kc.shape[0]
        s = _nt(q, kc)
        if mask is not None:
            s = jnp.where(mask[None], s.reshape(HEADS, tq, tk), NEG).reshape(HEADS * tq, tk)
        online(s, lambda p: _mm(p, kc[:, 0:KV_RANK]), mm_s, lm_s, am_s, slice(None))
        for p in range(HEADS // 2):
            s = _mm(qp[p], fkt[LANES * p:LANES * (p + 1), :]).reshape(2, tq, tk)
            s = s - ft[2 * p:2 * p + 2, :][:, None, :]
            if mask is not None:
                s = jnp.where(mask[None], s, NEG)
            online(s.reshape(2 * tq, tk), lambda pr: _nt(pr, fvt[LANES * p:LANES * (p + 1), :]),
                   mf_s, lf_s, af_s, p)

    meta_mask = lax.broadcasted_iota(jnp.int32, (tq, LANES), 1) < N_META
    update(mkc_ref[...], mfkt_ref[...], mfvt_ref[...], mft_ref[...], meta_mask)

    def body(kb, carry):
        off = pl.multiple_of(kb * tq, tq)
        update(kc_ref[0, pl.ds(off, tq), :], fkt_ref[0, :, pl.ds(off, tq)], fvt_ref[0, :, pl.ds(off, tq)],
               ft_ref[0, :, pl.ds(off, tq)], None)
        return carry

    lax.fori_loop(0, qi, body, 0)
    off = pl.multiple_of(qi * tq, tq)
    causal = (lax.broadcasted_iota(jnp.int32, (tq, tq), 1) <= lax.broadcasted_iota(jnp.int32, (tq, tq), 0))
    update(kc_ref[0, pl.ds(off, tq), :], fkt_ref[0, :, pl.ds(off, tq)], fvt_ref[0, :, pl.ds(off, tq)],
           ft_ref[0, :, pl.ds(off, tq)], causal)

    o = am_s[...] / lm_s[...]
    olat_o[0] = jnp.concatenate([o[hh * tq:(hh + 1) * tq] for hh in range(HEADS)], axis=1).astype(BF16)
    for p in range(HEADS // 2):
        o = af_s[p] / lf_s[p]
        of_o[0, :, LANES * p:LANES * (p + 1)] = jnp.where(lo, o[0:tq], o[tq:2 * tq]).astype(BF16)


def _suffix_kernel(pt_ref, *refs):
    n = SUF_PAGES
    lf_refs, u_ref, out_ref, carry_ref = refs[:n], refs[n], refs[n + 1], refs[n + 2]
    j = pl.program_id(1)

    @pl.when(j == 0)
    def _():
        carry_ref[...] = jnp.zeros(carry_ref.shape, F32)

    lfts, pieces = [], []
    for i in range(n):
        xt = lf_refs[i][0, 0]
        lfts.append(xt)
        hi, mid, lo = _split3(xt)
        pieces += [hi.astype(F32), mid.astype(F32), lo.astype(F32)]
    w_all = _mm(jnp.concatenate(pieces, axis=0).astype(BF16), u_ref[...])
    cur = carry_ref[:, 0:1]
    for i in reversed(range(n)):
        w = (w_all[24 * i:24 * i + 8] + w_all[24 * i + 8:24 * i + 16] + w_all[24 * i + 16:24 * i + 24])
        out_ref[0, :, LANES * i:LANES * (i + 1)] = w + cur
        cur = cur + w[:, 0:1] + lfts[i][:, 0:1]
    carry_ref[...] = jnp.broadcast_to(cur, carry_ref.shape)


def _decode_kernel(pt_ref, *refs):
    n = DEC_PAGES
    ckv_refs, kpet_refs = refs[0:n], refs[n:2 * n]
    fkt_refs, fvt_refs = refs[2 * n:3 * n], refs[3 * n:4 * n]
    (suf_ref, qlat_ref, qpe_ref, fq_ref, nckv_ref, nkpe_ref, nfk_ref, nfv_ref, gt_ref,
     olat_o, of_o, mm_s, lm_s, am_s, mf_s, lf_s, af_s, qbd_s) = refs[4 * n:]
    j = pl.program_id(1)
    rows = qlat_ref.shape[1]
    t_new = rows // HEADS

    def online(s, pv, m_ref, l_ref, a_ref):
        m_old = m_ref[...]
        m_new = jnp.maximum(m_old, jnp.max(s, axis=1, keepdims=True))
        alpha = jnp.exp(m_old - m_new)
        p = jnp.exp(s - m_new)
        l_ref[...] = alpha * l_ref[...] + jnp.sum(p, axis=1, keepdims=True)
        a_ref[...] = alpha * a_ref[...] + pv(p.astype(BF16))
        m_ref[...] = m_new

    @pl.when(j == 0)
    def _():
        mm_s[...] = jnp.full(mm_s.shape, NEG, F32)
        lm_s[...] = jnp.zeros(lm_s.shape, F32)
        am_s[...] = jnp.zeros(am_s.shape, F32)
        mf_s[...] = jnp.full(mf_s.shape, NEG, F32)
        lf_s[...] = jnp.zeros(lf_s.shape, F32)
        af_s[...] = jnp.zeros(af_s.shape, F32)
        fq = fq_ref[0].astype(F32)
        rep = jnp.concatenate([jnp.broadcast_to(fq[t:t + 1, :], (HEADS, D_FOX)) for t in range(t_new)], axis=0)
        row_head = lax.broadcasted_iota(jnp.int32, (rows, D_FOX), 0) % HEADS
        lane_head = lax.broadcasted_iota(jnp.int32, (rows, D_FOX), 1) // FOX_DIM
        qbd_s[...] = jnp.where(row_head == lane_head, rep, 0.0).astype(BF16)

        def pad(x):
            x = x.astype(F32)
            return jnp.concatenate([x, jnp.zeros((LANES - t_new, x.shape[1]), F32)], axis=0).astype(BF16)
        ckv, kpe, fk, fv = pad(nckv_ref[0, 0]), pad(nkpe_ref[0]), pad(nfk_ref[0]), pad(nfv_ref[0])
        key = lax.broadcasted_iota(jnp.int32, (rows, LANES), 1)
        tok = lax.broadcasted_iota(jnp.int32, (rows, LANES), 0) // HEADS
        mask = key <= tok
        s = jnp.where(mask, _nt(qlat_ref[0], ckv) + _nt(qpe_ref[0], kpe), NEG)
        online(s, lambda p: _mm(p, ckv), mm_s, lm_s, am_s)
        bias = jnp.concatenate([gt_ref[0]] * t_new, axis=0)
        s = jnp.where(mask, _nt(qbd_s[...], fk) - bias, NEG)
        online(s, lambda p: _mm(p, fv), mf_s, lf_s, af_s)

    ckv = jnp.concatenate([r[0, 0].astype(BF16) for r in ckv_refs], axis=0)
    kpet = jnp.concatenate([r[0, 0].astype(BF16) for r in kpet_refs], axis=1)
    fkt = jnp.concatenate([r[0, 0].astype(BF16) for r in fkt_refs], axis=1)
    fvt = jnp.concatenate([r[0, 0].astype(BF16) for r in fvt_refs], axis=1)
    s = _nt(qlat_ref[0], ckv) + _mm(qpe_ref[0], kpet)
    online(s, lambda p: _mm(p, ckv), mm_s, lm_s, am_s)
    bias = jnp.concatenate([suf_ref[0]] * t_new, axis=0)
    s = _mm(qbd_s[...], fkt) + bias
    online(s, lambda p: _nt(p, fvt), mf_s, lf_s, af_s)

    @pl.when(j == pl.num_programs(1) - 1)
    def _():
        olat_o[0] = (am_s[...] / lm_s[...]).astype(BF16)
        o = af_s[...] / lf_s[...]
        row_head = lax.broadcasted_iota(jnp.int32, (rows, D_FOX), 0) % HEADS
        lane_head = lax.broadcasted_iota(jnp.int32, (rows, D_FOX), 1) // FOX_DIM
        o = jnp.where(row_head == lane_head, o, 0.0)
        of_o[0] = jnp.concatenate(
            [jnp.sum(o[HEADS * t:HEADS * (t + 1)], axis=0, keepdims=True) for t in range(t_new)],
            axis=0).astype(BF16)


def _merge_kernel(olat_ref, of_ref, zs_ref, x_ref, wuv_ref, gom_ref, gof_ref, wo_ref, gpost_ref, y_o):
    o_m = _mm(olat_ref[...], wuv_ref[...])
    zs = zs_ref[...].astype(F32)
    y = jnp.concatenate([_rms(o_m, gom_ref[...]) * zs[:, 0:D_MLA],
                         _rms(of_ref[...].astype(F32), gof_ref[...]) * zs[:, D_MLA:]], axis=1)
    out = _mm(y.astype(BF16), wo_ref[...])
    y_o[...] = x_ref[...] + _rms(out, gpost_ref[...])


def _params(*sem):
    return pltpu.CompilerParams(dimension_semantics=sem, vmem_limit_bytes=VMEM_LIMIT)


def _full(shape):
    return pl.BlockSpec(shape, lambda *_: (0,) * len(shape))


def _prep_weights(w_in, w_uq, w_uk, w_uv):
    o = [0]
    for s in SPLITS:
        o.append(o[-1] + s)
    cq, ckv, kpe, zm, fq, fk, fv, fl, zf = [w_in[:, o[i]:o[i + 1]] for i in range(len(SPLITS))]
    half = MLA_ROPE // 2
    kpe_sw = jnp.concatenate([-kpe[:, half:], kpe[:, :half]], axis=1)
    w_in_p = jnp.concatenate([cq, ckv, jnp.tile(kpe, (1, 4)), jnp.tile(kpe_sw, (1, 4)), zm, fq, zf,
                              jnp.pad(fl, ((0, 0), (0, LANES - HEADS))), fk, fv], axis=1).astype(BF16)
    w_t = jnp.concatenate([fk, fv], axis=1).T.astype(BF16)
    wq = w_uq.reshape(Q_RANK, HEADS, MLA_NOPE + MLA_ROPE)
    pe = wq[:, :, MLA_NOPE:]
    pe_sw = jnp.concatenate([-pe[:, :, half:], pe[:, :, :half]], axis=2)
    w_uq_p = jnp.concatenate([wq[:, :, :MLA_NOPE].reshape(Q_RANK, -1), pe.reshape(Q_RANK, -1),
                              pe_sw.reshape(Q_RANK, -1)], axis=1).astype(BF16)
    eye = jnp.eye(HEADS, dtype=w_uk.dtype)
    w_uk_bd = jnp.einsum('rhd,hg->hdgr', w_uk, eye).reshape(HEADS * MLA_NOPE, HEADS * KV_RANK).astype(BF16)
    w_uv_bd = jnp.einsum('rhd,hg->hrgd', w_uv, eye).reshape(HEADS * KV_RANK, D_MLA).astype(BF16)
    return w_in_p, w_t, w_uq_p, w_uk_bd, w_uv_bd


def _rope_tables(pos):
    half = MLA_ROPE // 2
    inv = ROPE_THETA ** (-jnp.arange(half, dtype=F32) / half)
    ang = pos.astype(F32)[:, None] * inv[None, :]
    return jnp.tile(jnp.cos(ang), (1, LANES // half)), jnp.tile(jnp.sin(ang), (1, LANES // half))


def _proj_rows(x, cos, sin, u, wts, running):
    n = x.shape[0]
    tm = ROWS_TM
    row = lambda w: pl.BlockSpec((tm, w), lambda i: (i, 0))
    col = lambda h: pl.BlockSpec((h, tm), lambda i: (0, i))
    outs = ([(row(KV_RANK), (n, KV_RANK), F32)]
            + [(col(h), (h, n), F32) for h in (MLA_ROPE, D_FOX, D_FOX, HEADS, HEADS)]
            + [(row(2 * LANES), (n, 2 * LANES), BF16), (col(D_FOX), (D_FOX, n), BF16),
               (col(D_FOX), (D_FOX, n), BF16)]
            + [(row(w), (n, w), BF16) for w in (HEADS * KV_RANK, HEADS * MLA_ROPE, D_FOX, 2 * D_FOX,
                                                MLA_ROPE, D_FOX, D_FOX)])
    return pl.pallas_call(
        functools.partial(_proj_rows_kernel, running),
        grid=(n // tm,),
        in_specs=[row(D_MODEL), row(LANES), row(LANES)] + [_full(w.shape) for w in wts] + [_full(u.shape)],
        out_specs=[o[0] for o in outs],
        out_shape=[jax.ShapeDtypeStruct(o[1], o[2]) for o in outs],
        scratch_shapes=[pltpu.VMEM((HEADS, tm), F32)],
        compiler_params=_params("arbitrary"),
        name="proj_rows",
    )(x, cos, sin, *wts, u)


def _proj_prompt(x, cos, sin, meta, u, wts):
    b, s, _ = x.shape
    tm = PROJ_TM
    l = s + N_META
    chunk = lambda w: pl.BlockSpec((1, tm, w), lambda i, c: (i, c, 0))
    chunk_t = lambda h: pl.BlockSpec((1, h, tm), lambda i, c: (i, 0, c))
    tab = pl.BlockSpec((tm, LANES), lambda i, c: (c, 0))
    res_t = lambda h: pl.BlockSpec((1, 1, h, l), lambda i, c: (0, i, 0, 0))
    t_rows = [MLA_ROPE, D_FOX, D_FOX, HEADS]
    return pl.pallas_call(
        _proj_prompt_kernel,
        grid=(b, s // tm),
        in_specs=[chunk(D_MODEL), tab, tab] + [_full(m.shape) for m in meta]
        + [_full(w.shape) for w in wts] + [_full(u.shape)],
        out_specs=[pl.BlockSpec((1, 1, l, KV_RANK), lambda i, c: (0, i, 0, 0))] + [res_t(h) for h in t_rows]
        + [pl.BlockSpec((1, HEADS, tm, 2 * LANES), lambda i, c: (i, 0, c, 0)),
           chunk(2 * LANES), chunk(D_FOX), chunk_t(D_FOX), chunk_t(D_FOX), chunk(2 * D_FOX), chunk_t(HEADS)],
        out_shape=[jax.ShapeDtypeStruct((1, b, l, KV_RANK), F32)]
        + [jax.ShapeDtypeStruct((1, b, h, l), F32) for h in t_rows]
        + [jax.ShapeDtypeStruct((b, HEADS, s, 2 * LANES), BF16),
           jax.ShapeDtypeStruct((b, s, 2 * LANES), BF16), jax.ShapeDtypeStruct((b, s, D_FOX), BF16),
           jax.ShapeDtypeStruct((b, D_FOX, s), BF16), jax.ShapeDtypeStruct((b, D_FOX, s), BF16),
           jax.ShapeDtypeStruct((b, s, 2 * D_FOX), BF16), jax.ShapeDtypeStruct((b, HEADS, s), F32)],
        scratch_shapes=[pltpu.VMEM((HEADS, LANES), F32), pltpu.VMEM((MLA_ROPE, LANES), F32),
                        pltpu.VMEM((D_FOX, LANES), F32), pltpu.VMEM((D_FOX, LANES), F32),
                        pltpu.VMEM((HEADS, LANES), F32)],
        compiler_params=_params("arbitrary", "arbitrary"),
        name="proj_prompt",
    )(x, cos, sin, *meta, *wts, u)


def _attn_prompt(q, fq, kc, fkt16, fvt16, ft, mkc, mfkt16, mfvt16, mft):
    b, s, _ = fq.shape
    tq = ATT_TQ
    qblk = lambda w: pl.BlockSpec((1, tq, w), lambda i, j: (i, j, 0))
    seq_t = lambda h: pl.BlockSpec((1, h, s), lambda i, j: (i, 0, 0))
    return pl.pallas_call(
        _attn_prompt_kernel,
        grid=(b, s // tq),
        in_specs=[pl.BlockSpec((1, HEADS, tq, 2 * LANES), lambda i, j: (i, 0, j, 0)), qblk(D_FOX),
                  pl.BlockSpec((1, s, 2 * LANES), lambda i, j: (i, 0, 0)), seq_t(D_FOX), seq_t(D_FOX),
                  seq_t(HEADS),
                  _full(mkc.shape), _full(mfkt16.shape), _full(mfvt16.shape), _full(mft.shape)],
        out_specs=[qblk(HEADS * KV_RANK), qblk(D_FOX)],
        out_shape=[jax.ShapeDtypeStruct((b, s, HEADS * KV_RANK), BF16),
                   jax.ShapeDtypeStruct((b, s, D_FOX), BF16)],
        scratch_shapes=[pltpu.VMEM((HEADS * tq, 1), F32), pltpu.VMEM((HEADS * tq, 1), F32),
                        pltpu.VMEM((HEADS * tq, KV_RANK), F32),
                        pltpu.VMEM((HEADS // 2, 2 * tq, 1), F32), pltpu.VMEM((HEADS // 2, 2 * tq, 1), F32),
                        pltpu.VMEM((HEADS // 2, 2 * tq, LANES), F32)],
        compiler_params=_params("arbitrary", "arbitrary"),
        name="attn_prompt",
    )(q, fq, kc, fkt16, fvt16, ft, mkc, mfkt16, mfvt16, mft)


def _suffix(page_table, c_lft, u):
    bd, n_pages = page_table.shape
    n = SUF_PAGES
    steps = n_pages // n
    page = c_lft.shape[3]

    def lf_spec(i):
        return pl.BlockSpec((1, 1, HEADS, page),
                            lambda b, j, pt: (0, pt[b, n_pages - (j + 1) * n + i], 0, 0))
    return pl.pallas_call(
        _suffix_kernel,
        grid_spec=pltpu.PrefetchScalarGridSpec(
            num_scalar_prefetch=1, grid=(bd, steps),
            in_specs=[lf_spec(i) for i in range(n)] + [pl.BlockSpec(u.shape, lambda b, j, pt: (0, 0))],
            out_specs=pl.BlockSpec((1, HEADS, n * page), lambda b, j, pt: (b, 0, steps - 1 - j)),
            scratch_shapes=[pltpu.VMEM((HEADS, LANES), F32)]),
        out_shape=jax.ShapeDtypeStruct((bd, HEADS, n_pages * page), F32),
        compiler_params=_params("arbitrary", "arbitrary"),
        name="suffix",
    )(page_table, *([c_lft] * n), u)


def _decode(page_table, c_ckv, c_kpet, c_fkt, c_fvt, suf, qlat, qpe, fq, nckv, nkpe, nfk, nfv, gt):
    bd, n_pages = page_table.shape
    n = DEC_PAGES
    steps = n_pages // n
    page = c_ckv.shape[2]
    rows = qlat.shape[1]
    t_new = rows // HEADS

    def page_spec(shape, i):
        return pl.BlockSpec((1, 1) + shape, lambda b, j, pt: (0, pt[b, j * n + i], 0, 0))
    per_b = lambda shape: pl.BlockSpec((1,) + shape, lambda b, j, pt: (b,) + (0,) * len(shape))
    return pl.pallas_call(
        _decode_kernel,
        grid_spec=pltpu.PrefetchScalarGridSpec(
            num_scalar_prefetch=1, grid=(bd, steps),
            in_specs=[page_spec((page, KV_RANK), i) for i in range(n)]
            + [page_spec((MLA_ROPE, page), i) for i in range(n)]
            + [page_spec((D_FOX, page), i) for i in range(n)]
            + [page_spec((D_FOX, page), i) for i in range(n)]
            + [pl.BlockSpec((1, HEADS, n * page), lambda b, j, pt: (b, 0, j)),
               per_b((rows, KV_RANK)), per_b((rows, MLA_ROPE)), per_b((t_new, D_FOX)),
               pl.BlockSpec((1, 1, t_new, KV_RANK), lambda b, j, pt: (0, b, 0, 0)),
               per_b((t_new, MLA_ROPE)), per_b((t_new, D_FOX)), per_b((t_new, D_FOX)),
               per_b((HEADS, LANES))],
            out_specs=[per_b((rows, KV_RANK)), per_b((t_new, D_FOX))],
            scratch_shapes=[pltpu.VMEM((rows, 1), F32), pltpu.VMEM((rows, 1), F32),
                            pltpu.VMEM((rows, KV_RANK), F32),
                            pltpu.VMEM((rows, 1), F32), pltpu.VMEM((rows, 1), F32),
                            pltpu.VMEM((rows, D_FOX), F32), pltpu.VMEM((rows, D_FOX), BF16)]),
        out_shape=[jax.ShapeDtypeStruct((bd, rows, KV_RANK), BF16),
                   jax.ShapeDtypeStruct((bd, t_new, D_FOX), BF16)],
        compiler_params=_params("arbitrary", "arbitrary"),
        name="decode",
    )(page_table, *([c_ckv] * n), *([c_kpet] * n), *([c_fkt] * n), *([c_fvt] * n),
      suf, qlat, qpe, fq, nckv, nkpe, nfk, nfv, gt)


def _merge(olat, of, zs, x, w_uv_bd, g_om, g_of, w_o, g_post, tm):
    n = x.shape[0]
    row = lambda w: pl.BlockSpec((tm, w), lambda i: (i, 0))
    wts = (w_uv_bd, g_om, g_of, w_o, g_post)
    return pl.pallas_call(
        _merge_kernel,
        grid=(n // tm,),
        in_specs=[row(HEADS * KV_RANK), row(D_FOX), row(2 * D_FOX), row(D_MODEL)] + [_full(w.shape) for w in wts],
        out_specs=row(D_MODEL),
        out_shape=jax.ShapeDtypeStruct((n, D_MODEL), F32),
        compiler_params=_params("arbitrary"),
        name="merge",
    )(olat, of, zs, x, *wts)


def kernel(x_prompt, x_sample, cache_ckv, cache_kpe, cache_fox_k, cache_fox_v, cache_fox_logf, page_table,
           meta_tokens, g_pre, g_post, w_in, g_q, w_uq, g_kv, w_uk, w_uv, b_f, g_out_mla, g_out_fox, w_o):
    assert w_in.shape[0] == 1, "single-layer trunk only"
    b, s, d = x_prompt.shape
    bd, t_new, _ = x_sample.shape
    n_pool, page = cache_ckv.shape[1], cache_ckv.shape[2]
    n_pages = page_table.shape[1]
    past_len = n_pages * page
    l = s + N_META

    w_in_p, w_t, w_uq_p, w_uk_bd, w_uv_bd = _prep_weights(w_in[0], w_uq[0], w_uk[0], w_uv[0])
    b_f_row = jnp.pad(b_f[0][None, :], ((0, 0), (0, LANES - HEADS)))
    wts = (g_pre, w_in_p, g_q, w_uq_p, w_uk_bd, g_kv, b_f_row, w_t)
    w_o16 = w_o[0].astype(BF16)

    def tri(n):
        r = jnp.arange(n)
        return (r[:, None] <= r[None, :]).astype(BF16)

    x_meta = jnp.pad(meta_tokens.astype(F32), ((0, ROWS_TM - N_META), (0, 0)))
    cos_m, sin_m = _rope_tables(jnp.arange(ROWS_TM, dtype=jnp.int32))
    (mckv, mkpet, mfkt, mfvt, mlft, mft, mkc, mfkt16, mfvt16) = _proj_rows(
        x_meta, cos_m, sin_m, tri(ROWS_TM), wts, False)[:9]

    cos_p, sin_p = _rope_tables(N_META + jnp.arange(s, dtype=jnp.int32))
    (ckv_p, kpet_p, fkt_p, fvt_p, lft_p, q_p, kc_p, fq_p, fkt16_p, fvt16_p, zs_p, ft_p) = _proj_prompt(
        x_prompt, cos_p, sin_p, (mckv, mkpet, mfkt, mfvt, mlft, mft), tri(PROJ_TM), wts)
    olat_p, of_p = _attn_prompt(q_p, fq_p, kc_p, fkt16_p, fvt16_p, ft_p, mkc, mfkt16, mfvt16, mft)
    y_prompt = _merge(olat_p.reshape(b * s, -1), of_p.reshape(b * s, -1), zs_p.reshape(b * s, -1),
                      x_prompt.reshape(b * s, d), w_uv_bd, g_out_mla, g_out_fox, w_o16, g_post,
                      MERGE_TM).reshape(b, s, d)

    n_s = bd * t_new
    x_s = jnp.transpose(x_sample, (1, 0, 2)).reshape(n_s, d)
    pos_s = past_len + jnp.repeat(jnp.arange(t_new, dtype=jnp.int32), bd)
    cos_s, sin_s = _rope_tables(pos_s)
    (ckv_s, kpet_s, fkt_s, fvt_s, lft_s, gt_s, _, _, _, qlat_s, qpe_s, fq_s, zs_s, kpe16_s, fk16_s,
     fv16_s) = _proj_rows(x_s, cos_s, sin_s, tri(ROWS_TM), wts, True)

    def seq_major(a):
        return jnp.transpose(a.reshape(t_new, bd, a.shape[-1]), (1, 0, 2))
    ckv_s = seq_major(ckv_s)[None]
    gt = jnp.pad(jnp.transpose(gt_s.reshape(HEADS, t_new, bd), (2, 0, 1)),
                 ((0, 0), (0, 0), (0, LANES - t_new)))
    r = jnp.arange(page)
    suf = _suffix(page_table, jnp.transpose(cache_fox_logf, (0, 1, 3, 2)),
                  (r[:, None] > r[None, :]).astype(BF16))
    kt = lambda c: jnp.transpose(c, (0, 1, 3, 4, 2)).reshape(1, n_pool, D_FOX, page)
    olat_s, of_s = _decode(
        page_table, cache_ckv, jnp.transpose(cache_kpe, (0, 1, 3, 2)), kt(cache_fox_k), kt(cache_fox_v), suf,
        seq_major(qlat_s).reshape(bd, t_new * HEADS, KV_RANK),
        seq_major(qpe_s).reshape(bd, t_new * HEADS, MLA_ROPE), seq_major(fq_s),
        ckv_s, seq_major(kpe16_s), seq_major(fk16_s), seq_major(fv16_s), gt)
    y_sample = _merge(olat_s.reshape(n_s, -1), of_s.reshape(n_s, -1), seq_major(zs_s).reshape(n_s, -1),
                      x_sample.reshape(n_s, d), w_uv_bd, g_out_mla, g_out_fox, w_o16, g_post,
                      MERGE_TM).reshape(bd, t_new, d)

    def tok_minor(a):
        return jnp.transpose(a, (0, 1, 3, 2))

    def seq_minor(a):
        return jnp.transpose(a.reshape(a.shape[0], t_new, bd), (2, 1, 0))
    return (y_prompt, y_sample,
            ckv_p, tok_minor(kpet_p),
            tok_minor(fkt_p).reshape(1, b, l, HEADS, FOX_DIM), tok_minor(fvt_p).reshape(1, b, l, HEADS, FOX_DIM),
            tok_minor(lft_p),
            ckv_s, seq_minor(kpet_s)[None],
            seq_minor(fkt_s).reshape(1, bd, t_new, HEADS, FOX_DIM),
            seq_minor(fvt_s).reshape(1, bd, t_new, HEADS, FOX_DIM),
            seq_minor(lft_s)[None])
```

```python
import functools

import jax
import jax.numpy as jnp
from jax import lax
from jax.experimental import pallas as pl
from jax.experimental.pallas import tpu as pltpu

F32 = jnp.float32
BF16 = jnp.bfloat16

D_MODEL = 1024
N_META = 16
HEADS = 8
MLA_NOPE = 64
MLA_ROPE = 32
Q_RANK = 256
KV_RANK = 128
FOX_DIM = 64
D_FOX = HEADS * FOX_DIM
D_MLA = HEADS * 64
ROPE_THETA = 10000.0
EPS = 1e-6
NEG = -1e30
LOG2E = 1.4426950408889634
MLA_SCALE = (MLA_NOPE + MLA_ROPE) ** -0.5 * LOG2E
FOX_SCALE = FOX_DIM ** -0.5 * LOG2E
SPLITS = (Q_RANK, KV_RANK, MLA_ROPE, D_MLA, D_FOX, D_FOX, D_FOX, HEADS, D_FOX)

LANES = 128
C_CQ, C_CKV, C_KPE, C_KPESW, C_ZM, C_FQ, C_ZF, C_FL, C_FK, C_FV, C_END = (
    0, 256, 384, 512, 640, 1152, 1664, 2176, 2304, 2816, 3328)

PROJ_TM = 256
ROWS_TM = 128
ATT_TQ = 256
MERGE_TM = 512
SUF_PAGES = 32
DEC_PAGES = 16
VMEM_LIMIT = 56 * 1024 * 1024


def _nt(a, b):
    return lax.dot_general(a, b, (((1,), (1,)), ((), ())), preferred_element_type=F32)


def _mm(a, b):
    return jnp.dot(a, b, preferred_element_type=F32)


def _rms(x, g):
    return x * lax.rsqrt(jnp.mean(x * x, axis=-1, keepdims=True) + EPS) * g


def _split3(x):
    hi = x.astype(BF16)
    r1 = x - hi.astype(F32)
    mid = r1.astype(BF16)
    lo = (r1 - mid.astype(F32)).astype(BF16)
    return hi, mid, lo


def _mm_exact(x, u):
    hi, mid, lo = _split3(x)
    return _mm(hi, u) + _mm(mid, u) + _mm(lo, u)


def _log_sigmoid(x):
    return jnp.minimum(x, 0.0) - jnp.log(1.0 + jnp.exp(-jnp.abs(x)))


def _silu(x):
    return x / (1.0 + jnp.exp(-x))


def _proj_math(x, cos, sin, g_pre, w_a, g_q, w_uq, w_uk, g_kv, b_f):
    h = _rms(x, g_pre).astype(BF16)
    p = _mm(h, w_a)
    cqn = _rms(p[:, C_CQ:C_CKV], g_q).astype(BF16)
    qall = _mm(cqn, w_uq)
    cos2 = jnp.concatenate([cos, cos], axis=1)
    sin2 = jnp.concatenate([sin, sin], axis=1)
    q_pe = (qall[:, 512:768] * cos2 + qall[:, 768:1024] * sin2) * MLA_SCALE
    q_lat = _mm(qall[:, 0:512].astype(BF16), w_uk) * MLA_SCALE
    ckv = _rms(p[:, C_CKV:C_KPE], g_kv)
    kr4 = p[:, C_KPE:C_KPESW] * cos + p[:, C_KPESW:C_ZM] * sin
    zs = jnp.concatenate([_silu(p[:, C_ZM:C_FQ]), _silu(p[:, C_ZF:C_FL])], axis=1)
    fq = p[:, C_FQ:C_ZF] * FOX_SCALE
    lf = _log_sigmoid(p[:, C_FL:C_FK] + b_f)
    return h, p, dict(q_lat=q_lat, q_pe=q_pe, ckv=ckv, kr4=kr4, zs=zs, fq=fq, lf=lf)


def _proj_rows_kernel(running, x_ref, cos_ref, sin_ref, gpre_ref, win_ref, gq_ref, wuq_ref, wuk_ref,
                      gkv_ref, bf_ref, wt_ref, u_ref,
                      ckv_o, kpet_o, fkt_o, fvt_o, lft_o, ft_o, kc_o, fkt16_o, fvt16_o,
                      qlat_o, qpe_o, fq_o, zs_o, kpe_o, fk16_o, fv16_o, carry_ref):
    h, p, r = _proj_math(x_ref[...], cos_ref[...], sin_ref[...], gpre_ref[...], win_ref[...], gq_ref[...],
                         wuq_ref[...], wuk_ref[...], gkv_ref[...], bf_ref[...])
    pt = _nt(wt_ref[...], h)
    ckv_o[...] = r["ckv"]
    kpet_o[...] = r["kr4"].T[0:MLA_ROPE, :]
    fkt_o[...] = pt[0:D_FOX]
    fvt_o[...] = pt[D_FOX:2 * D_FOX]
    lft = r["lf"].T[0:HEADS, :]
    lft_o[...] = lft
    if running:
        @pl.when(pl.program_id(0) == 0)
        def _():
            carry_ref[...] = jnp.zeros(carry_ref.shape, F32)
        ft = carry_ref[...] + lft
        carry_ref[...] = ft
    else:
        ft = _mm_exact(lft, u_ref[...])
    ft_o[...] = ft
    kc_o[...] = jnp.concatenate([r["ckv"], r["kr4"]], axis=1).astype(BF16)
    fkt16_o[...] = pt[0:D_FOX].astype(BF16)
    fvt16_o[...] = pt[D_FOX:2 * D_FOX].astype(BF16)
    qlat_o[...] = r["q_lat"].astype(BF16)
    qpe_o[...] = r["q_pe"].astype(BF16)
    fq_o[...] = r["fq"].astype(BF16)
    zs_o[...] = r["zs"].astype(BF16)
    kpe_o[...] = r["kr4"][:, 0:MLA_ROPE].astype(BF16)
    fk16_o[...] = p[:, C_FK:C_FV].astype(BF16)
    fv16_o[...] = p[:, C_FV:C_END].astype(BF16)


def _proj_prompt_kernel(x_ref, cos_ref, sin_ref, mckv_ref, mkpet_ref, mfkt_ref, mfvt_ref, mlft_ref, mft_ref,
                        gpre_ref, win_ref, gq_ref, wuq_ref, wuk_ref, gkv_ref, bf_ref, wt_ref, u_ref,
                        ckv_o, kpet_o, fkt_o, fvt_o, lft_o,
                        q_o, kc_o, fq_o, fkt16_o, fvt16_o, zs_o, ft_o,
                        carry_ref, tail_kpe, tail_fk, tail_fv, tail_lf):
    c = pl.program_id(1)
    n_c = pl.num_programs(1)
    tm = x_ref.shape[1]

    @pl.when(c == 0)
    def _():
        ckv_o[0, 0, 0:N_META, :] = mckv_ref[0:N_META, :]
        tail_kpe[...] = mkpet_ref[...]
        tail_fk[...] = mfkt_ref[...]
        tail_fv[...] = mfvt_ref[...]
        tail_lf[...] = mlft_ref[...]
        carry_ref[...] = jnp.broadcast_to(mft_ref[:, N_META - 1:N_META], (HEADS, LANES))

    h, _, r = _proj_math(x_ref[0], cos_ref[...], sin_ref[...], gpre_ref[...], win_ref[:, 0:C_FK],
                         gq_ref[...], wuq_ref[...], wuk_ref[...], gkv_ref[...], bf_ref[...])
    pt = _nt(wt_ref[...], h)
    lft = r["lf"].T[0:HEADS, :]

    off = pl.multiple_of(N_META + c * tm, N_META)
    ckv_o[0, 0, pl.ds(off, tm), :] = r["ckv"]

    first = lax.broadcasted_iota(jnp.int32, (1, LANES), 1) < N_META

    def shifted_store(out_ref, tail_ref, cur):
        prev = tail_ref[...]
        for k in range(tm // LANES):
            rk = pltpu.roll(cur[:, LANES * k:LANES * (k + 1)], N_META, axis=1)
            col = pl.multiple_of(c * tm + LANES * k, LANES)
            out_ref[0, 0, :, pl.ds(col, LANES)] = jnp.where(first, prev, rk)
            prev = rk
        tail_ref[...] = prev

        @pl.when(c == n_c - 1)
        def _():
            end = out_ref.shape[3] - N_META
            out_ref[0, 0, :, end:end + N_META] = prev[:, 0:N_META]

    shifted_store(kpet_o, tail_kpe, r["kr4"].T[0:MLA_ROPE, :])
    shifted_store(fkt_o, tail_fk, pt[0:D_FOX])
    shifted_store(fvt_o, tail_fv, pt[D_FOX:2 * D_FOX])
    shifted_store(lft_o, tail_lf, lft)

    kc_o[0] = jnp.concatenate([r["ckv"], r["kr4"]], axis=1).astype(BF16)
    fkt16_o[0] = pt[0:D_FOX].astype(BF16)
    fvt16_o[0] = pt[D_FOX:2 * D_FOX].astype(BF16)
    fq_o[0] = r["fq"].astype(BF16)
    zs_o[0] = r["zs"].astype(BF16)
    group = lax.broadcasted_iota(jnp.int32, (1, LANES), 1) // MLA_ROPE
    for hh in range(HEADS):
        half = r["q_pe"][:, LANES * (hh // 4):LANES * (hh // 4 + 1)]
        pe = jnp.where(group == hh % 4, half, 0.0)
        q_o[0, hh] = jnp.concatenate([r["q_lat"][:, LANES * hh:LANES * (hh + 1)], pe], axis=1).astype(BF16)
    ft = carry_ref[:, 0:1] + _mm_exact(lft, u_ref[...])
    ft_o[0] = ft
    carry_ref[...] = jnp.broadcast_to(ft[:, tm - 1:tm], (HEADS, LANES))


def _attn_prompt_kernel(q_ref, fq_ref, kc_ref, fkt_ref, fvt_ref, ft_ref, mkc_ref, mfkt_ref, mfvt_ref, mft_ref,
                        olat_o, of_o, mm_s, am_s, *fox_s):
    mf_s, af_s = fox_s[0:HEADS // 2], fox_s[HEADS // 2:]
    tq = fq_ref.shape[1]
    qi = pl.program_id(1)
    q = q_ref[0].reshape(HEADS * tq, 2 * LANES)
    fq = fq_ref[0]
    lo = lax.broadcasted_iota(jnp.int32, (1, LANES), 1) < FOX_DIM
    zero = jnp.zeros((), BF16)
    qp = []
    for p in range(HEADS // 2):
        blk = fq[:, LANES * p:LANES * (p + 1)]
        qp.append(jnp.concatenate([jnp.where(lo, blk, zero), jnp.where(lo, zero, blk)], axis=0))

    for m_ref in (mm_s,) + tuple(mf_s):
        m_ref[...] = jnp.full(m_ref.shape, NEG, F32)
    for a_ref in (am_s,) + tuple(af_s):
        a_ref[...] = jnp.zeros(a_ref.shape, F32)

    def online(s, pv, m_ref, a_ref):
        tk = s.shape[1]
        m_old = m_ref[...]
        m_new = jnp.maximum(m_old, jnp.max(s, axis=1, keepdims=True))
        alpha = jnp.exp2(m_old - m_new)
        p = jnp.exp2(s - jnp.concatenate([m_new] * (tk // LANES), axis=1))
        a_ref[...] = jnp.concatenate([alpha, alpha], axis=1) * a_ref[...] + pv(p.astype(BF16))
        m_ref[...] = m_new

    def update(kc, fkt, fvt, ft, mask):
        tk = kc.shape[0]
        ft = ft * LOG2E
        s = _nt(q, kc)
        if mask is not None:
            s = jnp.where(mask[None], s.reshape(HEADS, tq, tk), NEG).reshape(HEADS * tq, tk)
        v1 = jnp.concatenate([kc[:, 0:KV_RANK], jnp.ones((tk, LANES), BF16)], axis=1)
        online(s, lambda pr: _mm(pr, v1), mm_s, am_s)
        for p in range(HEADS // 2):
            s = _mm(qp[p], fkt[LANES * p:LANES * (p + 1), :]).reshape(2, tq, tk)
            s = s - ft[2 * p:2 * p + 2, :][:, None, :]
            if mask is not None:
                s = jnp.where(mask[None], s, NEG)
            v1t = jnp.concatenate([fvt[LANES * p:LANES * (p + 1), :], jnp.ones((LANES, tk), BF16)], axis=0)
            online(s.reshape(2 * tq, tk), lambda pr, v1t=v1t: _nt(pr, v1t), mf_s[p], af_s[p])

    meta_mask = lax.broadcasted_iota(jnp.int32, (tq, LANES), 1) < N_META
    update(mkc_ref[...], mfkt_ref[...], mfvt_ref[...], mft_ref[...], meta_mask)

    def body(kb, carry):
        off = pl.multiple_of(kb * tq, tq)
        update(kc_ref[0, pl.ds(off, tq), :], fkt_ref[0, :, pl.ds(off, tq)], fvt_ref[0, :, pl.ds(off, tq)],
               ft_ref[0, :, pl.ds(off, tq)], None)
        return carry

    lax.fori_loop(0, qi, body, 0)
    off = pl.multiple_of(qi * tq, tq)
    causal = (lax.broadcasted_iota(jnp.int32, (tq, tq), 1) <= lax.broadcasted_iota(jnp.int32, (tq, tq), 0))
    update(kc_ref[0, pl.ds(off, tq), :], fkt_ref[0, :, pl.ds(off, tq)], fvt_ref[0, :, pl.ds(off, tq)],
           ft_ref[0, :, pl.ds(off, tq)], causal)

    a = am_s[...]
    o = a[:, 0:LANES] / a[:, LANES:2 * LANES]
    olat_o[0] = jnp.concatenate([o[hh * tq:(hh + 1) * tq] for hh in range(HEADS)], axis=1).astype(BF16)
    for p in range(HEADS // 2):
        a = af_s[p][...]
        o = a[:, 0:LANES] / a[:, LANES:2 * LANES]
        of_o[0, :, LANES * p:LANES * (p + 1)] = jnp.where(lo, o[0:tq], o[tq:2 * tq]).astype(BF16)


def _suffix_kernel(pt_ref, *refs):
    n = SUF_PAGES
    lf_refs, u_ref, out_ref, carry_ref = refs[:n], refs[n], refs[n + 1], refs[n + 2]
    j = pl.program_id(1)

    @pl.when(j == 0)
    def _():
        carry_ref[...] = jnp.zeros(carry_ref.shape, F32)

    lfts, pieces = [], []
    for i in range(n):
        xt = lf_refs[i][0, 0]
        lfts.append(xt)
        hi, mid, lo = _split3(xt)
        pieces += [hi.astype(F32), mid.astype(F32), lo.astype(F32)]
    w_all = _mm(jnp.concatenate(pieces, axis=0).astype(BF16), u_ref[...])
    cur = carry_ref[:, 0:1]
    for i in reversed(range(n)):
        w = (w_all[24 * i:24 * i + 8] + w_all[24 * i + 8:24 * i + 16] + w_all[24 * i + 16:24 * i + 24])
        out_ref[0, :, LANES * i:LANES * (i + 1)] = w + cur
        cur = cur + w[:, 0:1] + lfts[i][:, 0:1]
    carry_ref[...] = jnp.broadcast_to(cur, carry_ref.shape)


def _decode_kernel(pt_ref, *refs):
    n = DEC_PAGES
    ckv_refs, kpet_refs = refs[0:n], refs[n:2 * n]
    fkt_refs, fvt_refs = refs[2 * n:3 * n], refs[3 * n:4 * n]
    (suf_ref, qlat_ref, qpe_ref, fq_ref, nckv_ref, nkpe_ref, nfk_ref, nfv_ref, gt_ref,
     olat_o, of_o, mm_s, lm_s, am_s, mf_s, lf_s, af_s, qbd_s) = refs[4 * n:]
    j = pl.program_id(1)
    rows = qlat_ref.shape[1]
    t_new = rows // HEADS

    def online(s, pv, m_ref, l_ref, a_ref):
        m_old = m_ref[...]
        m_new = jnp.maximum(m_old, jnp.max(s, axis=1, keepdims=True))
        alpha = jnp.exp2(m_old - m_new)
        p = jnp.exp2(s - jnp.concatenate([m_new] * (s.shape[1] // LANES), axis=1))
        l_ref[...] = alpha * l_ref[...] + jnp.sum(p, axis=1, keepdims=True)
        a_ref[...] = jnp.concatenate([alpha] * (a_ref.shape[1] // LANES), axis=1) * a_ref[...] + pv(p.astype(BF16))
        m_ref[...] = m_new

    @pl.when(j == 0)
    def _():
        mm_s[...] = jnp.full(mm_s.shape, NEG, F32)
        lm_s[...] = jnp.zeros(lm_s.shape, F32)
        am_s[...] = jnp.zeros(am_s.shape, F32)
        mf_s[...] = jnp.full(mf_s.shape, NEG, F32)
        lf_s[...] = jnp.zeros(lf_s.shape, F32)
        af_s[...] = jnp.zeros(af_s.shape, F32)
        fq = fq_ref[0].astype(F32)
        rep = jnp.concatenate([jnp.broadcast_to(fq[t:t + 1, :], (HEADS, D_FOX)) for t in range(t_new)], axis=0)
        row_head = lax.broadcasted_iota(jnp.int32, (rows, D_FOX), 0) % HEADS
        lane_head = lax.broadcasted_iota(jnp.int32, (rows, D_FOX), 1) // FOX_DIM
        qbd_s[...] = jnp.where(row_head == lane_head, rep, 0.0).astype(BF16)

        def pad(x):
            x = x.astype(F32)
            return jnp.concatenate([x, jnp.zeros((LANES - t_new, x.shape[1]), F32)], axis=0).astype(BF16)
        ckv, kpe, fk, fv = pad(nckv_ref[0, 0]), pad(nkpe_ref[0]), pad(nfk_ref[0]), pad(nfv_ref[0])
        key = lax.broadcasted_iota(jnp.int32, (rows, LANES), 1)
        tok = lax.broadcasted_iota(jnp.int32, (rows, LANES), 0) // HEADS
        mask = key <= tok
        s = jnp.where(mask, _nt(qlat_ref[0], ckv) + _nt(qpe_ref[0], kpe), NEG)
        online(s, lambda p: _mm(p, ckv), mm_s, lm_s, am_s)
        bias = jnp.concatenate([gt_ref[0] * LOG2E] * t_new, axis=0)
        s = jnp.where(mask, _nt(qbd_s[...], fk) - bias, NEG)
        online(s, lambda p: _mm(p, fv), mf_s, lf_s, af_s)

    ckv = jnp.concatenate([r[0, 0].astype(BF16) for r in ckv_refs], axis=0)
    kpet = jnp.concatenate([r[0, 0].astype(BF16) for r in kpet_refs], axis=1)
    fkt = jnp.concatenate([r[0, 0].astype(BF16) for r in fkt_refs], axis=1)
    fvt = jnp.concatenate([r[0, 0].astype(BF16) for r in fvt_refs], axis=1)
    s = _nt(qlat_ref[0], ckv) + _mm(qpe_ref[0], kpet)
    online(s, lambda p: _mm(p, ckv), mm_s, lm_s, am_s)
    bias = jnp.concatenate([suf_ref[0] * LOG2E] * t_new, axis=0)
    s = _mm(qbd_s[...], fkt) + bias
    online(s, lambda p: _nt(p, fvt), mf_s, lf_s, af_s)

    @pl.when(j == pl.num_programs(1) - 1)
    def _():
        olat_o[0] = (am_s[...] / lm_s[...]).astype(BF16)
        o = af_s[...] / jnp.concatenate([lf_s[...]] * (D_FOX // LANES), axis=1)
        row_head = lax.broadcasted_iota(jnp.int32, (rows, D_FOX), 0) % HEADS
        lane_head = lax.broadcasted_iota(jnp.int32, (rows, D_FOX), 1) // FOX_DIM
        o = jnp.where(row_head == lane_head, o, 0.0)
        of_o[0] = jnp.concatenate(
            [jnp.sum(o[HEADS * t:HEADS * (t + 1)], axis=0, keepdims=True) for t in range(t_new)],
            axis=0).astype(BF16)


def _merge_kernel(olat_ref, of_ref, zs_ref, x_ref, wuv_ref, gom_ref, gof_ref, wo_ref, gpost_ref, y_o):
    o_m = _mm(olat_ref[...], wuv_ref[...])
    zs = zs_ref[...].astype(F32)
    y = jnp.concatenate([_rms(o_m, gom_ref[...]) * zs[:, 0:D_MLA],
                         _rms(of_ref[...].astype(F32), gof_ref[...]) * zs[:, D_MLA:]], axis=1)
    out = _mm(y.astype(BF16), wo_ref[...])
    y_o[...] = x_ref[...] + _rms(out, gpost_ref[...])


def _params(*sem):
    return pltpu.CompilerParams(dimension_semantics=sem, vmem_limit_bytes=VMEM_LIMIT)


def _full(shape):
    return pl.BlockSpec(shape, lambda *_: (0,) * len(shape))


def _prep_weights(w_in, w_uq, w_uk, w_uv):
    o = [0]
    for s in SPLITS:
        o.append(o[-1] + s)
    cq, ckv, kpe, zm, fq, fk, fv, fl, zf = [w_in[:, o[i]:o[i + 1]] for i in range(len(SPLITS))]
    half = MLA_ROPE // 2
    kpe_sw = jnp.concatenate([-kpe[:, half:], kpe[:, :half]], axis=1)
    w_in_p = jnp.concatenate([cq, ckv, jnp.tile(kpe, (1, 4)), jnp.tile(kpe_sw, (1, 4)), zm, fq, zf,
                              jnp.pad(fl, ((0, 0), (0, LANES - HEADS))), fk, fv], axis=1).astype(BF16)
    w_t = jnp.concatenate([fk, fv], axis=1).T.astype(BF16)
    wq = w_uq.reshape(Q_RANK, HEADS, MLA_NOPE + MLA_ROPE)
    pe = wq[:, :, MLA_NOPE:]
    pe_sw = jnp.concatenate([-pe[:, :, half:], pe[:, :, :half]], axis=2)
    w_uq_p = jnp.concatenate([wq[:, :, :MLA_NOPE].reshape(Q_RANK, -1), pe.reshape(Q_RANK, -1),
                              pe_sw.reshape(Q_RANK, -1)], axis=1).astype(BF16)
    eye = jnp.eye(HEADS, dtype=w_uk.dtype)
    w_uk_bd = jnp.einsum('rhd,hg->hdgr', w_uk, eye).reshape(HEADS * MLA_NOPE, HEADS * KV_RANK).astype(BF16)
    w_uv_bd = jnp.einsum('rhd,hg->hrgd', w_uv, eye).reshape(HEADS * KV_RANK, D_MLA).astype(BF16)
    return w_in_p, w_t, w_uq_p, w_uk_bd, w_uv_bd


def _rope_tables(pos):
    half = MLA_ROPE // 2
    inv = ROPE_THETA ** (-jnp.arange(half, dtype=F32) / half)
    ang = pos.astype(F32)[:, None] * inv[None, :]
    return jnp.tile(jnp.cos(ang), (1, LANES // half)), jnp.tile(jnp.sin(ang), (1, LANES // half))


def _proj_rows(x, cos, sin, u, wts, running):
    n = x.shape[0]
    tm = ROWS_TM
    row = lambda w: pl.BlockSpec((tm, w), lambda i: (i, 0))
    col = lambda h: pl.BlockSpec((h, tm), lambda i: (0, i))
    outs = ([(row(KV_RANK), (n, KV_RANK), F32)]
            + [(col(h), (h, n), F32) for h in (MLA_ROPE, D_FOX, D_FOX, HEADS, HEADS)]
            + [(row(2 * LANES), (n, 2 * LANES), BF16), (col(D_FOX), (D_FOX, n), BF16),
               (col(D_FOX), (D_FOX, n), BF16)]
            + [(row(w), (n, w), BF16) for w in (HEADS * KV_RANK, HEADS * MLA_ROPE, D_FOX, 2 * D_FOX,
                                                MLA_ROPE, D_FOX, D_FOX)])
    return pl.pallas_call(
        functools.partial(_proj_rows_kernel, running),
        grid=(n // tm,),
        in_specs=[row(D_MODEL), row(LANES), row(LANES)] + [_full(w.shape) for w in wts] + [_full(u.shape)],
        out_specs=[o[0] for o in outs],
        out_shape=[jax.ShapeDtypeStruct(o[1], o[2]) for o in outs],
        scratch_shapes=[pltpu.VMEM((HEADS, tm), F32)],
        compiler_params=_params("arbitrary"),
        name="proj_rows",
    )(x, cos, sin, *wts, u)


def _proj_prompt(x, cos, sin, meta, u, wts):
    b, s, _ = x.shape
    tm = PROJ_TM
    l = s + N_META
    chunk = lambda w: pl.BlockSpec((1, tm, w), lambda i, c: (i, c, 0))
    chunk_t = lambda h: pl.BlockSpec((1, h, tm), lambda i, c: (i, 0, c))
    tab = pl.BlockSpec((tm, LANES), lambda i, c: (c, 0))
    res_t = lambda h: pl.BlockSpec((1, 1, h, l), lambda i, c: (0, i, 0, 0))
    t_rows = [MLA_ROPE, D_FOX, D_FOX, HEADS]
    return pl.pallas_call(
        _proj_prompt_kernel,
        grid=(b, s // tm),
        in_specs=[chunk(D_MODEL), tab, tab] + [_full(m.shape) for m in meta]
        + [_full(w.shape) for w in wts] + [_full(u.shape)],
        out_specs=[pl.BlockSpec((1, 1, l, KV_RANK), lambda i, c: (0, i, 0, 0))] + [res_t(h) for h in t_rows]
        + [pl.BlockSpec((1, HEADS, tm, 2 * LANES), lambda i, c: (i, 0, c, 0)),
           chunk(2 * LANES), chunk(D_FOX), chunk_t(D_FOX), chunk_t(D_FOX), chunk(2 * D_FOX), chunk_t(HEADS)],
        out_shape=[jax.ShapeDtypeStruct((1, b, l, KV_RANK), F32)]
        + [jax.ShapeDtypeStruct((1, b, h, l), F32) for h in t_rows]
        + [jax.ShapeDtypeStruct((b, HEADS, s, 2 * LANES), BF16),
           jax.ShapeDtypeStruct((b, s, 2 * LANES), BF16), jax.ShapeDtypeStruct((b, s, D_FOX), BF16),
           jax.ShapeDtypeStruct((b, D_FOX, s), BF16), jax.ShapeDtypeStruct((b, D_FOX, s), BF16),
           jax.ShapeDtypeStruct((b, s, 2 * D_FOX), BF16), jax.ShapeDtypeStruct((b, HEADS, s), F32)],
        scratch_shapes=[pltpu.VMEM((HEADS, LANES), F32), pltpu.VMEM((MLA_ROPE, LANES), F32),
                        pltpu.VMEM((D_FOX, LANES), F32), pltpu.VMEM((D_FOX, LANES), F32),
                        pltpu.VMEM((HEADS, LANES), F32)],
        compiler_params=_params("arbitrary", "arbitrary"),
        name="proj_prompt",
    )(x, cos, sin, *meta, *wts, u)


def _attn_prompt(q, fq, kc, fkt16, fvt16, ft, mkc, mfkt16, mfvt16, mft):
    b, s, _ = fq.shape
    tq = ATT_TQ
    qblk = lambda w: pl.BlockSpec((1, tq, w), lambda i, j: (i, j, 0))
    seq_t = lambda h: pl.BlockSpec((1, h, s), lambda i, j: (i, 0, 0))
    return pl.pallas_call(
        _attn_prompt_kernel,
        grid=(b, s // tq),
        in_specs=[pl.BlockSpec((1, HEADS, tq, 2 * LANES), lambda i, j: (i, 0, j, 0)), qblk(D_FOX),
                  pl.BlockSpec((1, s, 2 * LANES), lambda i, j: (i, 0, 0)), seq_t(D_FOX), seq_t(D_FOX),
                  seq_t(HEADS),
                  _full(mkc.shape), _full(mfkt16.shape), _full(mfvt16.shape), _full(mft.shape)],
        out_specs=[qblk(HEADS * KV_RANK), qblk(D_FOX)],
        out_shape=[jax.ShapeDtypeStruct((b, s, HEADS * KV_RANK), BF16),
                   jax.ShapeDtypeStruct((b, s, D_FOX), BF16)],
        scratch_shapes=[pltpu.VMEM((HEADS * tq, LANES), F32), pltpu.VMEM((HEADS * tq, 2 * LANES), F32)]
        + [pltpu.VMEM((2 * tq, LANES), F32)] * (HEADS // 2)
        + [pltpu.VMEM((2 * tq, 2 * LANES), F32)] * (HEADS // 2),
        compiler_params=_params("arbitrary", "arbitrary"),
        name="attn_prompt",
    )(q, fq, kc, fkt16, fvt16, ft, mkc, mfkt16, mfvt16, mft)


def _suffix(page_table, c_lft, u):
    bd, n_pages = page_table.shape
    n = SUF_PAGES
    steps = n_pages // n
    page = c_lft.shape[3]

    def lf_spec(i):
        return pl.BlockSpec((1, 1, HEADS, page),
                            lambda b, j, pt: (0, pt[b, n_pages - (j + 1) * n + i], 0, 0))
    return pl.pallas_call(
        _suffix_kernel,
        grid_spec=pltpu.PrefetchScalarGridSpec(
            num_scalar_prefetch=1, grid=(bd, steps),
            in_specs=[lf_spec(i) for i in range(n)] + [pl.BlockSpec(u.shape, lambda b, j, pt: (0, 0))],
            out_specs=pl.BlockSpec((1, HEADS, n * page), lambda b, j, pt: (b, 0, steps - 1 - j)),
            scratch_shapes=[pltpu.VMEM((HEADS, LANES), F32)]),
        out_shape=jax.ShapeDtypeStruct((bd, HEADS, n_pages * page), F32),
        compiler_params=_params("arbitrary", "arbitrary"),
        name="suffix",
    )(page_table, *([c_lft] * n), u)


def _decode(page_table, c_ckv, c_kpet, c_fkt, c_fvt, suf, qlat, qpe, fq, nckv, nkpe, nfk, nfv, gt):
    bd, n_pages = page_table.shape
    n = DEC_PAGES
    steps = n_pages // n
    page = c_ckv.shape[2]
    rows = qlat.shape[1]
    t_new = rows // HEADS

    def page_spec(shape, i):
        return pl.BlockSpec((1, 1) + shape, lambda b, j, pt: (0, pt[b, j * n + i], 0, 0))
    per_b = lambda shape: pl.BlockSpec((1,) + shape, lambda b, j, pt: (b,) + (0,) * len(shape))
    return pl.pallas_call(
        _decode_kernel,
        grid_spec=pltpu.PrefetchScalarGridSpec(
            num_scalar_prefetch=1, grid=(bd, steps),
            in_specs=[page_spec((page, KV_RANK), i) for i in range(n)]
            + [page_spec((MLA_ROPE, page), i) for i in range(n)]
            + [page_spec((D_FOX, page), i) for i in range(n)]
            + [page_spec((D_FOX, page), i) for i in range(n)]
            + [pl.BlockSpec((1, HEADS, n * page), lambda b, j, pt: (b, 0, j)),
               per_b((rows, KV_RANK)), per_b((rows, MLA_ROPE)), per_b((t_new, D_FOX)),
               pl.BlockSpec((1, 1, t_new, KV_RANK), lambda b, j, pt: (0, b, 0, 0)),
               per_b((t_new, MLA_ROPE)), per_b((t_new, D_FOX)), per_b((t_new, D_FOX)),
               per_b((HEADS, LANES))],
            out_specs=[per_b((rows, KV_RANK)), per_b((t_new, D_FOX))],
            scratch_shapes=[pltpu.VMEM((rows, LANES), F32), pltpu.VMEM((rows, LANES), F32),
                            pltpu.VMEM((rows, KV_RANK), F32),
                            pltpu.VMEM((rows, LANES), F32), pltpu.VMEM((rows, LANES), F32),
                            pltpu.VMEM((rows, D_FOX), F32), pltpu.VMEM((rows, D_FOX), BF16)]),
        out_shape=[jax.ShapeDtypeStruct((bd, rows, KV_RANK), BF16),
                   jax.ShapeDtypeStruct((bd, t_new, D_FOX), BF16)],
        compiler_params=_params("arbitrary", "arbitrary"),
        name="decode",
    )(page_table, *([c_ckv] * n), *([c_kpet] * n), *([c_fkt] * n), *([c_fvt] * n),
      suf, qlat, qpe, fq, nckv, nkpe, nfk, nfv, gt)


def _merge(olat, of, zs, x, w_uv_bd, g_om, g_of, w_o, g_post, tm):
    n = x.shape[0]
    row = lambda w: pl.BlockSpec((tm, w), lambda i: (i, 0))
    wts = (w_uv_bd, g_om, g_of, w_o, g_post)
    return pl.pallas_call(
        _merge_kernel,
        grid=(n // tm,),
        in_specs=[row(HEADS * KV_RANK), row(D_FOX), row(2 * D_FOX), row(D_MODEL)] + [_full(w.shape) for w in wts],
        out_specs=row(D_MODEL),
        out_shape=jax.ShapeDtypeStruct((n, D_MODEL), F32),
        compiler_params=_params("arbitrary"),
        name="merge",
    )(olat, of, zs, x, *wts)


def kernel(x_prompt, x_sample, cache_ckv, cache_kpe, cache_fox_k, cache_fox_v, cache_fox_logf, page_table,
           meta_tokens, g_pre, g_post, w_in, g_q, w_uq, g_kv, w_uk, w_uv, b_f, g_out_mla, g_out_fox, w_o):
    assert w_in.shape[0] == 1, "single-layer trunk only"
    b, s, d = x_prompt.shape
    bd, t_new, _ = x_sample.shape
    n_pool, page = cache_ckv.shape[1], cache_ckv.shape[2]
    n_pages = page_table.shape[1]
    past_len = n_pages * page
    l = s + N_META

    w_in_p, w_t, w_uq_p, w_uk_bd, w_uv_bd = _prep_weights(w_in[0], w_uq[0], w_uk[0], w_uv[0])
    b_f_row = jnp.pad(b_f[0][None, :], ((0, 0), (0, LANES - HEADS)))
    wts = (g_pre, w_in_p, g_q, w_uq_p, w_uk_bd, g_kv, b_f_row, w_t)
    w_o16 = w_o[0].astype(BF16)

    def tri(n):
        r = jnp.arange(n)
        return (r[:, None] <= r[None, :]).astype(BF16)

    x_meta = jnp.pad(meta_tokens.astype(F32), ((0, ROWS_TM - N_META), (0, 0)))
    cos_m, sin_m = _rope_tables(jnp.arange(ROWS_TM, dtype=jnp.int32))
    (mckv, mkpet, mfkt, mfvt, mlft, mft, mkc, mfkt16, mfvt16) = _proj_rows(
        x_meta, cos_m, sin_m, tri(ROWS_TM), wts, False)[:9]

    cos_p, sin_p = _rope_tables(N_META + jnp.arange(s, dtype=jnp.int32))
    (ckv_p, kpet_p, fkt_p, fvt_p, lft_p, q_p, kc_p, fq_p, fkt16_p, fvt16_p, zs_p, ft_p) = _proj_prompt(
        x_prompt, cos_p, sin_p, (mckv, mkpet, mfkt, mfvt, mlft, mft), tri(PROJ_TM), wts)
    olat_p, of_p = _attn_prompt(q_p, fq_p, kc_p, fkt16_p, fvt16_p, ft_p, mkc, mfkt16, mfvt16, mft)
    y_prompt = _merge(olat_p.reshape(b * s, -1), of_p.reshape(b * s, -1), zs_p.reshape(b * s, -1),
                      x_prompt.reshape(b * s, d), w_uv_bd, g_out_mla, g_out_fox, w_o16, g_post,
                      MERGE_TM).reshape(b, s, d)

    n_s = bd * t_new
    x_s = jnp.transpose(x_sample, (1, 0, 2)).reshape(n_s, d)
    pos_s = past_len + jnp.repeat(jnp.arange(t_new, dtype=jnp.int32), bd)
    cos_s, sin_s = _rope_tables(pos_s)
    (ckv_s, kpet_s, fkt_s, fvt_s, lft_s, gt_s, _, _, _, qlat_s, qpe_s, fq_s, zs_s, kpe16_s, fk16_s,
     fv16_s) = _proj_rows(x_s, cos_s, sin_s, tri(ROWS_TM), wts, True)

    def seq_major(a):
        return jnp.transpose(a.reshape(t_new, bd, a.shape[-1]), (1, 0, 2))
    ckv_s = seq_major(ckv_s)[None]
    gt = jnp.pad(jnp.transpose(gt_s.reshape(HEADS, t_new, bd), (2, 0, 1)),
                 ((0, 0), (0, 0), (0, LANES - t_new)))
    r = jnp.arange(page)
    suf = _suffix(page_table, jnp.transpose(cache_fox_logf, (0, 1, 3, 2)),
                  (r[:, None] > r[None, :]).astype(BF16))
    kt = lambda c: jnp.transpose(c, (0, 1, 3, 4, 2)).reshape(1, n_pool, D_FOX, page)
    olat_s, of_s = _decode(
        page_table, cache_ckv, jnp.transpose(cache_kpe, (0, 1, 3, 2)), kt(cache_fox_k), kt(cache_fox_v), suf,
        seq_major(qlat_s).reshape(bd, t_new * HEADS, KV_RANK),
        seq_major(qpe_s).reshape(bd, t_new * HEADS, MLA_ROPE), seq_major(fq_s),
        ckv_s, seq_major(kpe16_s), seq_major(fk16_s), seq_major(fv16_s), gt)
    y_sample = _merge(olat_s.reshape(n_s, -1), of_s.reshape(n_s, -1), seq_major(zs_s).reshape(n_s, -1),
                      x_sample.reshape(n_s, d), w_uv_bd, g_out_mla, g_out_fox, w_o16, g_post,
                      MERGE_TM).reshape(bd, t_new, d)

    def tok_minor(a):
        return jnp.transpose(a, (0, 1, 3, 2))

    def seq_minor(a):
        return jnp.transpose(a.reshape(a.shape[0], t_new, bd), (2, 1, 0))
    return (y_prompt, y_sample,
            ckv_p, tok_minor(kpet_p),
            tok_minor(fkt_p).reshape(1, b, l, HEADS, FOX_DIM), tok_minor(fvt_p).reshape(1, b, l, HEADS, FOX_DIM),
            tok_minor(lft_p),
            ckv_s, seq_minor(kpet_s)[None],
            seq_minor(fkt_s).reshape(1, bd, t_new, HEADS, FOX_DIM),
            seq_minor(fvt_s).reshape(1, bd, t_new, HEADS, FOX_DIM),
            seq_minor(lft_s)[None])
```

```python
import functools

import jax
import jax.numpy as jnp
from jax import lax
from jax.experimental import pallas as pl
from jax.experimental.pallas import tpu as pltpu

F32 = jnp.float32
BF16 = jnp.bfloat16

D_MODEL = 1024
N_META = 16
HEADS = 8
MLA_NOPE = 64
MLA_ROPE = 32
Q_RANK = 256
KV_RANK = 128
FOX_DIM = 64
D_FOX = HEADS * FOX_DIM
D_MLA = HEADS * 64
ROPE_THETA = 10000.0
EPS = 1e-6
NEG = -1e30
LOG2E = 1.4426950408889634
MLA_SCALE = (MLA_NOPE + MLA_ROPE) ** -0.5 * LOG2E
FOX_SCALE = FOX_DIM ** -0.5 * LOG2E
SPLITS = (Q_RANK, KV_RANK, MLA_ROPE, D_MLA, D_FOX, D_FOX, D_FOX, HEADS, D_FOX)

LANES = 128
C_CQ, C_CKV, C_KPE, C_KPESW, C_ZM, C_FQ, C_ZF, C_FL, C_FK, C_FV, C_END = (
    0, 256, 384, 512, 640, 1152, 1664, 2176, 2304, 2816, 3328)

PROJ_TM = 256
ROWS_TM = 128
ATT_TQ = 256
MERGE_TM = 512
SUF_PAGES = 32
DEC_PAGES = 16
VMEM_LIMIT = 56 * 1024 * 1024


def _nt(a, b):
    return lax.dot_general(a, b, (((1,), (1,)), ((), ())), preferred_element_type=F32)


def _mm(a, b):
    return jnp.dot(a, b, preferred_element_type=F32)


def _rms(x, g):
    return x * lax.rsqrt(jnp.mean(x * x, axis=-1, keepdims=True) + EPS) * g


def _split3(x):
    hi = x.astype(BF16)
    r1 = x - hi.astype(F32)
    mid = r1.astype(BF16)
    lo = (r1 - mid.astype(F32)).astype(BF16)
    return hi, mid, lo


def _mm_exact(x, u):
    hi, mid, lo = _split3(x)
    return _mm(hi, u) + _mm(mid, u) + _mm(lo, u)


def _log_sigmoid(x):
    return jnp.minimum(x, 0.0) - jnp.log(1.0 + jnp.exp(-jnp.abs(x)))


def _silu(x):
    return x / (1.0 + jnp.exp(-x))


def _proj_math(x, cos, sin, g_pre, w_a, g_q, w_uq, w_uk, g_kv, b_f):
    h = _rms(x, g_pre).astype(BF16)
    p = _mm(h, w_a)
    cqn = _rms(p[:, C_CQ:C_CKV], g_q).astype(BF16)
    qall = _mm(cqn, w_uq)
    cos2 = jnp.concatenate([cos, cos], axis=1)
    sin2 = jnp.concatenate([sin, sin], axis=1)
    q_pe = (qall[:, 512:768] * cos2 + qall[:, 768:1024] * sin2) * MLA_SCALE
    q_lat = _mm(qall[:, 0:512].astype(BF16), w_uk) * MLA_SCALE
    ckv = _rms(p[:, C_CKV:C_KPE], g_kv)
    kr4 = p[:, C_KPE:C_KPESW] * cos + p[:, C_KPESW:C_ZM] * sin
    zs = jnp.concatenate([_silu(p[:, C_ZM:C_FQ]), _silu(p[:, C_ZF:C_FL])], axis=1)
    fq = p[:, C_FQ:C_ZF] * FOX_SCALE
    lf = _log_sigmoid(p[:, C_FL:C_FK] + b_f)
    return h, p, dict(q_lat=q_lat, q_pe=q_pe, ckv=ckv, kr4=kr4, zs=zs, fq=fq, lf=lf)


def _proj_rows_kernel(running, x_ref, cos_ref, sin_ref, gpre_ref, win_ref, gq_ref, wuq_ref, wuk_ref,
                      gkv_ref, bf_ref, wt_ref, u_ref,
                      ckv_o, kpet_o, fkt_o, fvt_o, lft_o, ft_o, kc_o, fkt16_o, fvt16_o,
                      qlat_o, qpe_o, fq_o, zs_o, kpe_o, fk16_o, fv16_o, carry_ref):
    h, p, r = _proj_math(x_ref[...], cos_ref[...], sin_ref[...], gpre_ref[...], win_ref[...], gq_ref[...],
                         wuq_ref[...], wuk_ref[...], gkv_ref[...], bf_ref[...])
    pt = _nt(wt_ref[...], h)
    ckv_o[...] = r["ckv"]
    kpet_o[...] = r["kr4"].T[0:MLA_ROPE, :]
    fkt_o[...] = pt[0:D_FOX]
    fvt_o[...] = pt[D_FOX:2 * D_FOX]
    lft = r["lf"].T[0:HEADS, :]
    lft_o[...] = lft
    if running:
        @pl.when(pl.program_id(0) == 0)
        def _():
            carry_ref[...] = jnp.zeros(carry_ref.shape, F32)
        ft = carry_ref[...] + lft
        carry_ref[...] = ft
    else:
        ft = _mm_exact(lft, u_ref[...])
    ft_o[...] = ft
    kc_o[...] = jnp.concatenate([r["ckv"], r["kr4"]], axis=1).astype(BF16)
    fkt16_o[...] = pt[0:D_FOX].astype(BF16)
    fvt16_o[...] = pt[D_FOX:2 * D_FOX].astype(BF16)
    qlat_o[...] = r["q_lat"].astype(BF16)
    qpe_o[...] = r["q_pe"].astype(BF16)
    fq_o[...] = r["fq"].astype(BF16)
    zs_o[...] = r["zs"].astype(BF16)
    kpe_o[...] = r["kr4"][:, 0:MLA_ROPE].astype(BF16)
    fk16_o[...] = p[:, C_FK:C_FV].astype(BF16)
    fv16_o[...] = p[:, C_FV:C_END].astype(BF16)


def _proj_prompt_kernel(x_ref, cos_ref, sin_ref, mckv_ref, mkpet_ref, mfkt_ref, mfvt_ref, mlft_ref, mft_ref,
                        gpre_ref, win_ref, gq_ref, wuq_ref, wuk_ref, gkv_ref, bf_ref, wt_ref, u_ref,
                        ckv_o, kpet_o, fkt_o, fvt_o, lft_o,
                        q_o, kc_o, fq_o, fkt16_o, fvt16_o, zs_o, ft_o,
                        carry_ref, tail_kpe, tail_fk, tail_fv, tail_lf):
    c = pl.program_id(1)
    n_c = pl.num_programs(1)
    tm = x_ref.shape[1]

    @pl.when(c == 0)
    def _():
        ckv_o[0, 0, 0:N_META, :] = mckv_ref[0:N_META, :]
        tail_kpe[...] = mkpet_ref[...]
        tail_fk[...] = mfkt_ref[...]
        tail_fv[...] = mfvt_ref[...]
        tail_lf[...] = mlft_ref[...]
        carry_ref[...] = jnp.broadcast_to(mft_ref[:, N_META - 1:N_META], (HEADS, LANES))

    h, _, r = _proj_math(x_ref[0], cos_ref[...], sin_ref[...], gpre_ref[...], win_ref[:, 0:C_FK],
                         gq_ref[...], wuq_ref[...], wuk_ref[...], gkv_ref[...], bf_ref[...])
    pt = _nt(wt_ref[...], h)
    lft = r["lf"].T[0:HEADS, :]

    off = pl.multiple_of(N_META + c * tm, N_META)
    ckv_o[0, 0, pl.ds(off, tm), :] = r["ckv"]

    first = lax.broadcasted_iota(jnp.int32, (1, LANES), 1) < N_META

    def shifted_store(out_ref, tail_ref, cur):
        prev = tail_ref[...]
        for k in range(tm // LANES):
            rk = pltpu.roll(cur[:, LANES * k:LANES * (k + 1)], N_META, axis=1)
            col = pl.multiple_of(c * tm + LANES * k, LANES)
            out_ref[0, 0, :, pl.ds(col, LANES)] = jnp.where(first, prev, rk)
            prev = rk
        tail_ref[...] = prev

        @pl.when(c == n_c - 1)
        def _():
            end = out_ref.shape[3] - N_META
            out_ref[0, 0, :, end:end + N_META] = prev[:, 0:N_META]

    shifted_store(kpet_o, tail_kpe, r["kr4"].T[0:MLA_ROPE, :])
    shifted_store(fkt_o, tail_fk, pt[0:D_FOX])
    shifted_store(fvt_o, tail_fv, pt[D_FOX:2 * D_FOX])
    shifted_store(lft_o, tail_lf, lft)

    kc_o[0] = jnp.concatenate([r["ckv"], r["kr4"]], axis=1).astype(BF16)
    fkt16_o[0] = pt[0:D_FOX].astype(BF16)
    fvt16_o[0] = pt[D_FOX:2 * D_FOX].astype(BF16)
    fq_o[0] = r["fq"].astype(BF16)
    zs_o[0] = r["zs"].astype(BF16)
    group = lax.broadcasted_iota(jnp.int32, (1, LANES), 1) // MLA_ROPE
    for hh in range(HEADS):
        half = r["q_pe"][:, LANES * (hh // 4):LANES * (hh // 4 + 1)]
        pe = jnp.where(group == hh % 4, half, 0.0)
        q_o[0, hh] = jnp.concatenate([r["q_lat"][:, LANES * hh:LANES * (hh + 1)], pe], axis=1).astype(BF16)
    ft = carry_ref[:, 0:1] + _mm_exact(lft, u_ref[...])
    ft_o[0] = ft
    carry_ref[...] = jnp.broadcast_to(ft[:, tm - 1:tm], (HEADS, LANES))


def _attn_prompt_kernel(q_ref, fq_ref, kc_ref, fkt_ref, fvt_ref, ft_ref, mkc_ref, mfkt_ref, mfvt_ref, mft_ref,
                        olat_o, of_o, mm_s, am_s, *fox_s):
    mf_s, af_s = fox_s[0:HEADS // 2], fox_s[HEADS // 2:]
    tq = fq_ref.shape[1]
    qi = pl.program_id(1)
    q = q_ref[0].reshape(HEADS * tq, 2 * LANES)
    fq = fq_ref[0]
    lo = lax.broadcasted_iota(jnp.int32, (1, LANES), 1) < FOX_DIM
    zero = jnp.zeros((), BF16)
    qp = []
    for p in range(HEADS // 2):
        blk = fq[:, LANES * p:LANES * (p + 1)]
        qp.append(jnp.concatenate([jnp.where(lo, blk, zero), jnp.where(lo, zero, blk)], axis=0))

    for m_ref in (mm_s,) + tuple(mf_s):
        m_ref[...] = jnp.full(m_ref.shape, NEG, F32)
    for a_ref in (am_s,) + tuple(af_s):
        a_ref[...] = jnp.zeros(a_ref.shape, F32)

    def online(s, pv, m_ref, a_ref):
        tk = s.shape[1]
        m_old = m_ref[...]
        m_new = jnp.maximum(m_old, jnp.max(s, axis=1, keepdims=True))
        alpha = jnp.exp2(m_old - m_new)
        p = jnp.exp2(s - jnp.concatenate([m_new] * (tk // LANES), axis=1))
        a_ref[...] = jnp.concatenate([alpha, alpha], axis=1) * a_ref[...] + pv(p.astype(BF16))
        m_ref[...] = m_new

    def update(kc, fkt, fvt, ft, mask):
        tk = kc.shape[0]
        ft = ft * LOG2E
        s = _nt(q, kc)
        if mask is not None:
            s = jnp.where(mask[None], s.reshape(HEADS, tq, tk), NEG).reshape(HEADS * tq, tk)
        v1 = jnp.concatenate([kc[:, 0:KV_RANK], jnp.ones((tk, LANES), BF16)], axis=1)
        online(s, lambda pr: _mm(pr, v1), mm_s, am_s)
        for p in range(HEADS // 2):
            s = _mm(qp[p], fkt[LANES * p:LANES * (p + 1), :]).reshape(2, tq, tk)
            s = s - ft[2 * p:2 * p + 2, :][:, None, :]
            if mask is not None:
                s = jnp.where(mask[None], s, NEG)
            v1t = jnp.concatenate([fvt[LANES * p:LANES * (p + 1), :], jnp.ones((LANES, tk), BF16)], axis=0)
            online(s.reshape(2 * tq, tk), lambda pr, v1t=v1t: _nt(pr, v1t), mf_s[p], af_s[p])

    meta_mask = lax.broadcasted_iota(jnp.int32, (tq, LANES), 1) < N_META
    update(mkc_ref[...], mfkt_ref[...], mfvt_ref[...], mft_ref[...], meta_mask)

    def body(kb, carry):
        off = pl.multiple_of(kb * tq, tq)
        update(kc_ref[0, pl.ds(off, tq), :], fkt_ref[0, :, pl.ds(off, tq)], fvt_ref[0, :, pl.ds(off, tq)],
               ft_ref[0, :, pl.ds(off, tq)], None)
        return carry

    lax.fori_loop(0, qi, body, 0)
    off = pl.multiple_of(qi * tq, tq)
    causal = (lax.broadcasted_iota(jnp.int32, (tq, tq), 1) <= lax.broadcasted_iota(jnp.int32, (tq, tq), 0))
    update(kc_ref[0, pl.ds(off, tq), :], fkt_ref[0, :, pl.ds(off, tq)], fvt_ref[0, :, pl.ds(off, tq)],
           ft_ref[0, :, pl.ds(off, tq)], causal)

    a = am_s[...]
    o = a[:, 0:LANES] / a[:, LANES:2 * LANES]
    olat_o[0] = jnp.concatenate([o[hh * tq:(hh + 1) * tq] for hh in range(HEADS)], axis=1).astype(BF16)
    for p in range(HEADS // 2):
        a = af_s[p][...]
        o = a[:, 0:LANES] / a[:, LANES:2 * LANES]
        of_o[0, :, LANES * p:LANES * (p + 1)] = jnp.where(lo, o[0:tq], o[tq:2 * tq]).astype(BF16)


def _paged_fetch(copies_for_page, n_pages):
    b, j = pl.program_id(0), pl.program_id(1)
    n_b, n_j = pl.num_programs(0), pl.num_programs(1)
    step = b * n_j + j
    slot = step % 2

    def for_pages(bb, jj, sl, start):
        def body(i, carry):
            for cp in copies_for_page(bb, jj, sl, i):
                if start:
                    cp.start()
                else:
                    cp.wait()
            return carry
        lax.fori_loop(0, n_pages, body, 0)

    @pl.when(step == 0)
    def _():
        for_pages(b, j, slot, True)

    @pl.when(step + 1 < n_b * n_j)
    def _():
        wrap = j + 1 == n_j
        for_pages(jnp.where(wrap, b + 1, b), jnp.where(wrap, 0, j + 1), 1 - slot, True)

    return slot, lambda: for_pages(b, j, slot, False)


def _suffix_kernel(pt_ref, u_ref, lft_hbm, out_ref, buf, sem, carry_ref):
    n = SUF_PAGES
    n_pages = pt_ref.shape[1]
    j = pl.program_id(1)

    def copies(bb, jj, sl, i):
        pid = pt_ref[bb, n_pages - (jj + 1) * n + i]
        rows = pl.ds(pl.multiple_of(i * HEADS, HEADS), HEADS)
        return [pltpu.make_async_copy(lft_hbm.at[0, pid], buf.at[sl, rows, :], sem.at[sl])]

    slot, wait = _paged_fetch(copies, n)

    @pl.when(j == 0)
    def _():
        carry_ref[...] = jnp.zeros(carry_ref.shape, F32)

    wait()
    hi, mid, lo = _split3(buf[slot])
    w_all = _mm(hi, u_ref[...]) + _mm(mid, u_ref[...]) + _mm(lo, u_ref[...])
    cur = carry_ref[...]
    for i in reversed(range(n)):
        w = w_all[HEADS * i:HEADS * (i + 1)]
        out_ref[0, :, LANES * i:LANES * (i + 1)] = w[:, 0:LANES] + cur
        cur = cur + w[:, LANES:2 * LANES]
    carry_ref[...] = cur


def _decode_kernel(pt_ref, suf_ref, qlat_ref, qpe_ref, fq_ref, nckv_ref, nkpe_ref, nfk_ref, nfv_ref, gt_ref,
                   ckv_hbm, kpet_hbm, fkt_hbm, fvt_hbm, olat_o, of_o,
                   ckv_buf, kpet_buf, fkt_buf, fvt_buf, sems, mm_s, lm_s, am_s, mf_s, lf_s, af_s, qbd_s):
    n = DEC_PAGES
    page = ckv_hbm.shape[2]
    j = pl.program_id(1)
    rows = qlat_ref.shape[1]
    t_new = rows // HEADS

    def copies(bb, jj, sl, i):
        pid = pt_ref[bb, jj * n + i]
        keys = pl.ds(pl.multiple_of(i * page, page), page)
        return [pltpu.make_async_copy(ckv_hbm.at[0, pid], ckv_buf.at[sl, keys, :], sems.at[sl, 0]),
                pltpu.make_async_copy(kpet_hbm.at[0, pid], kpet_buf.at[sl, :, keys], sems.at[sl, 1]),
                pltpu.make_async_copy(fkt_hbm.at[0, pid], fkt_buf.at[sl, :, keys], sems.at[sl, 2]),
                pltpu.make_async_copy(fvt_hbm.at[0, pid], fvt_buf.at[sl, :, keys], sems.at[sl, 3])]

    slot, wait = _paged_fetch(copies, n)

    def online(s, pv, m_ref, l_ref, a_ref):
        m_old = m_ref[...]
        m_new = jnp.maximum(m_old, jnp.max(s, axis=1, keepdims=True))
        alpha = jnp.exp2(m_old - m_new)
        p = jnp.exp2(s - jnp.concatenate([m_new] * (s.shape[1] // LANES), axis=1))
        l_ref[...] = alpha * l_ref[...] + jnp.sum(p, axis=1, keepdims=True)
        a_ref[...] = jnp.concatenate([alpha] * (a_ref.shape[1] // LANES), axis=1) * a_ref[...] + pv(p.astype(BF16))
        m_ref[...] = m_new

    @pl.when(j == 0)
    def _():
        mm_s[...] = jnp.full(mm_s.shape, NEG, F32)
        lm_s[...] = jnp.zeros(lm_s.shape, F32)
        am_s[...] = jnp.zeros(am_s.shape, F32)
        mf_s[...] = jnp.full(mf_s.shape, NEG, F32)
        lf_s[...] = jnp.zeros(lf_s.shape, F32)
        af_s[...] = jnp.zeros(af_s.shape, F32)
        fq = fq_ref[0].astype(F32)
        rep = jnp.concatenate([jnp.broadcast_to(fq[t:t + 1, :], (HEADS, D_FOX)) for t in range(t_new)], axis=0)
        row_head = lax.broadcasted_iota(jnp.int32, (rows, D_FOX), 0) % HEADS
        lane_head = lax.broadcasted_iota(jnp.int32, (rows, D_FOX), 1) // FOX_DIM
        qbd_s[...] = jnp.where(row_head == lane_head, rep, 0.0).astype(BF16)

        def pad(x):
            x = x.astype(F32)
            return jnp.concatenate([x, jnp.zeros((LANES - t_new, x.shape[1]), F32)], axis=0).astype(BF16)
        ckv, kpe, fk, fv = pad(nckv_ref[0, 0]), pad(nkpe_ref[0]), pad(nfk_ref[0]), pad(nfv_ref[0])
        key = lax.broadcasted_iota(jnp.int32, (rows, LANES), 1)
        tok = lax.broadcasted_iota(jnp.int32, (rows, LANES), 0) // HEADS
        mask = key <= tok
        s = jnp.where(mask, _nt(qlat_ref[0], ckv) + _nt(qpe_ref[0], kpe), NEG)
        online(s, lambda p: _mm(p, ckv), mm_s, lm_s, am_s)
        bias = jnp.concatenate([gt_ref[0] * LOG2E] * t_new, axis=0)
        s = jnp.where(mask, _nt(qbd_s[...], fk) - bias, NEG)
        online(s, lambda p: _mm(p, fv), mf_s, lf_s, af_s)

    wait()
    ckv = ckv_buf[slot].astype(BF16)
    kpet = kpet_buf[slot].astype(BF16)
    fkt = fkt_buf[slot].astype(BF16)
    fvt = fvt_buf[slot].astype(BF16)
    s = _nt(qlat_ref[0], ckv) + _mm(qpe_ref[0], kpet)
    online(s, lambda p: _mm(p, ckv), mm_s, lm_s, am_s)
    bias = jnp.concatenate([suf_ref[0] * LOG2E] * t_new, axis=0)
    s = _mm(qbd_s[...], fkt) + bias
    online(s, lambda p: _nt(p, fvt), mf_s, lf_s, af_s)

    @pl.when(j == pl.num_programs(1) - 1)
    def _():
        olat_o[0] = (am_s[...] / lm_s[...]).astype(BF16)
        o = af_s[...] / jnp.concatenate([lf_s[...]] * (D_FOX // LANES), axis=1)
        row_head = lax.broadcasted_iota(jnp.int32, (rows, D_FOX), 0) % HEADS
        lane_head = lax.broadcasted_iota(jnp.int32, (rows, D_FOX), 1) // FOX_DIM
        o = jnp.where(row_head == lane_head, o, 0.0)
        of_o[0] = jnp.concatenate(
            [jnp.sum(o[HEADS * t:HEADS * (t + 1)], axis=0, keepdims=True) for t in range(t_new)],
            axis=0).astype(BF16)


def _merge_kernel(olat_ref, of_ref, zs_ref, x_ref, wuv_ref, gom_ref, gof_ref, wo_ref, gpost_ref, y_o):
    o_m = _mm(olat_ref[...], wuv_ref[...])
    zs = zs_ref[...].astype(F32)
    y = jnp.concatenate([_rms(o_m, gom_ref[...]) * zs[:, 0:D_MLA],
                         _rms(of_ref[...].astype(F32), gof_ref[...]) * zs[:, D_MLA:]], axis=1)
    out = _mm(y.astype(BF16), wo_ref[...])
    y_o[...] = x_ref[...] + _rms(out, gpost_ref[...])


def _params(*sem):
    return pltpu.CompilerParams(dimension_semantics=sem, vmem_limit_bytes=VMEM_LIMIT)


def _full(shape):
    return pl.BlockSpec(shape, lambda *_: (0,) * len(shape))


def _prep_weights(w_in, w_uq, w_uk, w_uv):
    o = [0]
    for s in SPLITS:
        o.append(o[-1] + s)
    cq, ckv, kpe, zm, fq, fk, fv, fl, zf = [w_in[:, o[i]:o[i + 1]] for i in range(len(SPLITS))]
    half = MLA_ROPE // 2
    kpe_sw = jnp.concatenate([-kpe[:, half:], kpe[:, :half]], axis=1)
    w_in_p = jnp.concatenate([cq, ckv, jnp.tile(kpe, (1, 4)), jnp.tile(kpe_sw, (1, 4)), zm, fq, zf,
                              jnp.pad(fl, ((0, 0), (0, LANES - HEADS))), fk, fv], axis=1).astype(BF16)
    w_t = jnp.concatenate([fk, fv], axis=1).T.astype(BF16)
    wq = w_uq.reshape(Q_RANK, HEADS, MLA_NOPE + MLA_ROPE)
    pe = wq[:, :, MLA_NOPE:]
    pe_sw = jnp.concatenate([-pe[:, :, half:], pe[:, :, :half]], axis=2)
    w_uq_p = jnp.concatenate([wq[:, :, :MLA_NOPE].reshape(Q_RANK, -1), pe.reshape(Q_RANK, -1),
                              pe_sw.reshape(Q_RANK, -1)], axis=1).astype(BF16)
    eye = jnp.eye(HEADS, dtype=w_uk.dtype)
    w_uk_bd = jnp.einsum('rhd,hg->hdgr', w_uk, eye).reshape(HEADS * MLA_NOPE, HEADS * KV_RANK).astype(BF16)
    w_uv_bd = jnp.einsum('rhd,hg->hrgd', w_uv, eye).reshape(HEADS * KV_RANK, D_MLA).astype(BF16)
    return w_in_p, w_t, w_uq_p, w_uk_bd, w_uv_bd


def _rope_tables(pos):
    half = MLA_ROPE // 2
    inv = ROPE_THETA ** (-jnp.arange(half, dtype=F32) / half)
    ang = pos.astype(F32)[:, None] * inv[None, :]
    return jnp.tile(jnp.cos(ang), (1, LANES // half)), jnp.tile(jnp.sin(ang), (1, LANES // half))


def _proj_rows(x, cos, sin, u, wts, running):
    n = x.shape[0]
    tm = ROWS_TM
    row = lambda w: pl.BlockSpec((tm, w), lambda i: (i, 0))
    col = lambda h: pl.BlockSpec((h, tm), lambda i: (0, i))
    outs = ([(row(KV_RANK), (n, KV_RANK), F32)]
            + [(col(h), (h, n), F32) for h in (MLA_ROPE, D_FOX, D_FOX, HEADS, HEADS)]
            + [(row(2 * LANES), (n, 2 * LANES), BF16), (col(D_FOX), (D_FOX, n), BF16),
               (col(D_FOX), (D_FOX, n), BF16)]
            + [(row(w), (n, w), BF16) for w in (HEADS * KV_RANK, HEADS * MLA_ROPE, D_FOX, 2 * D_FOX,
                                                MLA_ROPE, D_FOX, D_FOX)])
    return pl.pallas_call(
        functools.partial(_proj_rows_kernel, running),
        grid=(n // tm,),
        in_specs=[row(D_MODEL), row(LANES), row(LANES)] + [_full(w.shape) for w in wts] + [_full(u.shape)],
        out_specs=[o[0] for o in outs],
        out_shape=[jax.ShapeDtypeStruct(o[1], o[2]) for o in outs],
        scratch_shapes=[pltpu.VMEM((HEADS, tm), F32)],
        compiler_params=_params("arbitrary"),
        name="proj_rows",
    )(x, cos, sin, *wts, u)


def _proj_prompt(x, cos, sin, meta, u, wts):
    b, s, _ = x.shape
    tm = PROJ_TM
    l = s + N_META
    chunk = lambda w: pl.BlockSpec((1, tm, w), lambda i, c: (i, c, 0))
    chunk_t = lambda h: pl.BlockSpec((1, h, tm), lambda i, c: (i, 0, c))
    tab = pl.BlockSpec((tm, LANES), lambda i, c: (c, 0))
    res_t = lambda h: pl.BlockSpec((1, 1, h, l), lambda i, c: (0, i, 0, 0))
    t_rows = [MLA_ROPE, D_FOX, D_FOX, HEADS]
    return pl.pallas_call(
        _proj_prompt_kernel,
        grid=(b, s // tm),
        in_specs=[chunk(D_MODEL), tab, tab] + [_full(m.shape) for m in meta]
        + [_full(w.shape) for w in wts] + [_full(u.shape)],
        out_specs=[pl.BlockSpec((1, 1, l, KV_RANK), lambda i, c: (0, i, 0, 0))] + [res_t(h) for h in t_rows]
        + [pl.BlockSpec((1, HEADS, tm, 2 * LANES), lambda i, c: (i, 0, c, 0)),
           chunk(2 * LANES), chunk(D_FOX), chunk_t(D_FOX), chunk_t(D_FOX), chunk(2 * D_FOX), chunk_t(HEADS)],
        out_shape=[jax.ShapeDtypeStruct((1, b, l, KV_RANK), F32)]
        + [jax.ShapeDtypeStruct((1, b, h, l), F32) for h in t_rows]
        + [jax.ShapeDtypeStruct((b, HEADS, s, 2 * LANES), BF16),
           jax.ShapeDtypeStruct((b, s, 2 * LANES), BF16), jax.ShapeDtypeStruct((b, s, D_FOX), BF16),
           jax.ShapeDtypeStruct((b, D_FOX, s), BF16), jax.ShapeDtypeStruct((b, D_FOX, s), BF16),
           jax.ShapeDtypeStruct((b, s, 2 * D_FOX), BF16), jax.ShapeDtypeStruct((b, HEADS, s), F32)],
        scratch_shapes=[pltpu.VMEM((HEADS, LANES), F32), pltpu.VMEM((MLA_ROPE, LANES), F32),
                        pltpu.VMEM((D_FOX, LANES), F32), pltpu.VMEM((D_FOX, LANES), F32),
                        pltpu.VMEM((HEADS, LANES), F32)],
        compiler_params=_params("arbitrary", "arbitrary"),
        name="proj_prompt",
    )(x, cos, sin, *meta, *wts, u)


def _attn_prompt(q, fq, kc, fkt16, fvt16, ft, mkc, mfkt16, mfvt16, mft):
    b, s, _ = fq.shape
    tq = ATT_TQ
    qblk = lambda w: pl.BlockSpec((1, tq, w), lambda i, j: (i, j, 0))
    seq_t = lambda h: pl.BlockSpec((1, h, s), lambda i, j: (i, 0, 0))
    return pl.pallas_call(
        _attn_prompt_kernel,
        grid=(b, s // tq),
        in_specs=[pl.BlockSpec((1, HEADS, tq, 2 * LANES), lambda i, j: (i, 0, j, 0)), qblk(D_FOX),
                  pl.BlockSpec((1, s, 2 * LANES), lambda i, j: (i, 0, 0)), seq_t(D_FOX), seq_t(D_FOX),
                  seq_t(HEADS),
                  _full(mkc.shape), _full(mfkt16.shape), _full(mfvt16.shape), _full(mft.shape)],
        out_specs=[qblk(HEADS * KV_RANK), qblk(D_FOX)],
        out_shape=[jax.ShapeDtypeStruct((b, s, HEADS * KV_RANK), BF16),
                   jax.ShapeDtypeStruct((b, s, D_FOX), BF16)],
        scratch_shapes=[pltpu.VMEM((HEADS * tq, LANES), F32), pltpu.VMEM((HEADS * tq, 2 * LANES), F32)]
        + [pltpu.VMEM((2 * tq, LANES), F32)] * (HEADS // 2)
        + [pltpu.VMEM((2 * tq, 2 * LANES), F32)] * (HEADS // 2),
        compiler_params=_params("arbitrary", "arbitrary"),
        name="attn_prompt",
    )(q, fq, kc, fkt16, fvt16, ft, mkc, mfkt16, mfvt16, mft)


def _suffix(page_table, c_lft, u):
    bd, n_pages = page_table.shape
    n = SUF_PAGES
    steps = n_pages // n
    page = c_lft.shape[3]
    return pl.pallas_call(
        _suffix_kernel,
        grid_spec=pltpu.PrefetchScalarGridSpec(
            num_scalar_prefetch=1, grid=(bd, steps),
            in_specs=[pl.BlockSpec(u.shape, lambda b, j, pt: (0, 0)), pl.BlockSpec(memory_space=pl.ANY)],
            out_specs=pl.BlockSpec((1, HEADS, n * page), lambda b, j, pt: (b, 0, steps - 1 - j)),
            scratch_shapes=[pltpu.VMEM((2, n * HEADS, page), F32), pltpu.SemaphoreType.DMA((2,)),
                            pltpu.VMEM((HEADS, LANES), F32)]),
        out_shape=jax.ShapeDtypeStruct((bd, HEADS, n_pages * page), F32),
        compiler_params=_params("arbitrary", "arbitrary"),
        name="suffix",
    )(page_table, u, c_lft)


def _decode(page_table, c_ckv, c_kpet, c_fkt, c_fvt, suf, qlat, qpe, fq, nckv, nkpe, nfk, nfv, gt):
    bd, n_pages = page_table.shape
    n = DEC_PAGES
    steps = n_pages // n
    page = c_ckv.shape[2]
    rows = qlat.shape[1]
    t_new = rows // HEADS

    keys = n * page
    per_b = lambda shape: pl.BlockSpec((1,) + shape, lambda b, j, pt: (b,) + (0,) * len(shape))
    hbm = pl.BlockSpec(memory_space=pl.ANY)
    return pl.pallas_call(
        _decode_kernel,
        grid_spec=pltpu.PrefetchScalarGridSpec(
            num_scalar_prefetch=1, grid=(bd, steps),
            in_specs=[pl.BlockSpec((1, HEADS, keys), lambda b, j, pt: (b, 0, j)),
                      per_b((rows, KV_RANK)), per_b((rows, MLA_ROPE)), per_b((t_new, D_FOX)),
                      pl.BlockSpec((1, 1, t_new, KV_RANK), lambda b, j, pt: (0, b, 0, 0)),
                      per_b((t_new, MLA_ROPE)), per_b((t_new, D_FOX)), per_b((t_new, D_FOX)),
                      per_b((HEADS, LANES)), hbm, hbm, hbm, hbm],
            out_specs=[per_b((rows, KV_RANK)), per_b((t_new, D_FOX))],
            scratch_shapes=[pltpu.VMEM((2, keys, KV_RANK), F32), pltpu.VMEM((2, MLA_ROPE, keys), F32),
                            pltpu.VMEM((2, D_FOX, keys), F32), pltpu.VMEM((2, D_FOX, keys), F32),
                            pltpu.SemaphoreType.DMA((2, 4)),
                            pltpu.VMEM((rows, LANES), F32), pltpu.VMEM((rows, LANES), F32),
                            pltpu.VMEM((rows, KV_RANK), F32),
                            pltpu.VMEM((rows, LANES), F32), pltpu.VMEM((rows, LANES), F32),
                            pltpu.VMEM((rows, D_FOX), F32), pltpu.VMEM((rows, D_FOX), BF16)]),
        out_shape=[jax.ShapeDtypeStruct((bd, rows, KV_RANK), BF16),
                   jax.ShapeDtypeStruct((bd, t_new, D_FOX), BF16)],
        compiler_params=_params("arbitrary", "arbitrary"),
        name="decode",
    )(page_table, suf, qlat, qpe, fq, nckv, nkpe, nfk, nfv, gt, c_ckv, c_kpet, c_fkt, c_fvt)


def _merge(olat, of, zs, x, w_uv_bd, g_om, g_of, w_o, g_post, tm):
    n = x.shape[0]
    row = lambda w: pl.BlockSpec((tm, w), lambda i: (i, 0))
    wts = (w_uv_bd, g_om, g_of, w_o, g_post)
    return pl.pallas_call(
        _merge_kernel,
        grid=(n // tm,),
        in_specs=[row(HEADS * KV_RANK), row(D_FOX), row(2 * D_FOX), row(D_MODEL)] + [_full(w.shape) for w in wts],
        out_specs=row(D_MODEL),
        out_shape=jax.ShapeDtypeStruct((n, D_MODEL), F32),
        compiler_params=_params("arbitrary"),
        name="merge",
    )(olat, of, zs, x, *wts)


def kernel(x_prompt, x_sample, cache_ckv, cache_kpe, cache_fox_k, cache_fox_v, cache_fox_logf, page_table,
           meta_tokens, g_pre, g_post, w_in, g_q, w_uq, g_kv, w_uk, w_uv, b_f, g_out_mla, g_out_fox, w_o):
    assert w_in.shape[0] == 1, "single-layer trunk only"
    b, s, d = x_prompt.shape
    bd, t_new, _ = x_sample.shape
    n_pool, page = cache_ckv.shape[1], cache_ckv.shape[2]
    n_pages = page_table.shape[1]
    past_len = n_pages * page
    l = s + N_META

    w_in_p, w_t, w_uq_p, w_uk_bd, w_uv_bd = _prep_weights(w_in[0], w_uq[0], w_uk[0], w_uv[0])
    b_f_row = jnp.pad(b_f[0][None, :], ((0, 0), (0, LANES - HEADS)))
    wts = (g_pre, w_in_p, g_q, w_uq_p, w_uk_bd, g_kv, b_f_row, w_t)
    w_o16 = w_o[0].astype(BF16)

    def tri(n):
        r = jnp.arange(n)
        return (r[:, None] <= r[None, :]).astype(BF16)

    x_meta = jnp.pad(meta_tokens.astype(F32), ((0, ROWS_TM - N_META), (0, 0)))
    cos_m, sin_m = _rope_tables(jnp.arange(ROWS_TM, dtype=jnp.int32))
    (mckv, mkpet, mfkt, mfvt, mlft, mft, mkc, mfkt16, mfvt16) = _proj_rows(
        x_meta, cos_m, sin_m, tri(ROWS_TM), wts, False)[:9]

    cos_p, sin_p = _rope_tables(N_META + jnp.arange(s, dtype=jnp.int32))
    (ckv_p, kpet_p, fkt_p, fvt_p, lft_p, q_p, kc_p, fq_p, fkt16_p, fvt16_p, zs_p, ft_p) = _proj_prompt(
        x_prompt, cos_p, sin_p, (mckv, mkpet, mfkt, mfvt, mlft, mft), tri(PROJ_TM), wts)
    olat_p, of_p = _attn_prompt(q_p, fq_p, kc_p, fkt16_p, fvt16_p, ft_p, mkc, mfkt16, mfvt16, mft)
    y_prompt = _merge(olat_p.reshape(b * s, -1), of_p.reshape(b * s, -1), zs_p.reshape(b * s, -1),
                      x_prompt.reshape(b * s, d), w_uv_bd, g_out_mla, g_out_fox, w_o16, g_post,
                      MERGE_TM).reshape(b, s, d)

    n_s = bd * t_new
    x_s = jnp.transpose(x_sample, (1, 0, 2)).reshape(n_s, d)
    pos_s = past_len + jnp.repeat(jnp.arange(t_new, dtype=jnp.int32), bd)
    cos_s, sin_s = _rope_tables(pos_s)
    (ckv_s, kpet_s, fkt_s, fvt_s, lft_s, gt_s, _, _, _, qlat_s, qpe_s, fq_s, zs_s, kpe16_s, fk16_s,
     fv16_s) = _proj_rows(x_s, cos_s, sin_s, tri(ROWS_TM), wts, True)

    def seq_major(a):
        return jnp.transpose(a.reshape(t_new, bd, a.shape[-1]), (1, 0, 2))
    ckv_s = seq_major(ckv_s)[None]
    gt = jnp.pad(jnp.transpose(gt_s.reshape(HEADS, t_new, bd), (2, 0, 1)),
                 ((0, 0), (0, 0), (0, LANES - t_new)))
    r = jnp.arange(page)
    later = jnp.concatenate([r[:, None] > r[None, :], jnp.ones((page, LANES), bool)], axis=1).astype(BF16)
    suf = _suffix(page_table, jnp.transpose(cache_fox_logf, (0, 1, 3, 2)), later)
    kt = lambda c: jnp.transpose(c, (0, 1, 3, 4, 2)).reshape(1, n_pool, D_FOX, page)
    olat_s, of_s = _decode(
        page_table, cache_ckv, jnp.transpose(cache_kpe, (0, 1, 3, 2)), kt(cache_fox_k), kt(cache_fox_v), suf,
        seq_major(qlat_s).reshape(bd, t_new * HEADS, KV_RANK),
        seq_major(qpe_s).reshape(bd, t_new * HEADS, MLA_ROPE), seq_major(fq_s),
        ckv_s, seq_major(kpe16_s), seq_major(fk16_s), seq_major(fv16_s), gt)
    y_sample = _merge(olat_s.reshape(n_s, -1), of_s.reshape(n_s, -1), seq_major(zs_s).reshape(n_s, -1),
                      x_sample.reshape(n_s, d), w_uv_bd, g_out_mla, g_out_fox, w_o16, g_post,
                      MERGE_TM).reshape(bd, t_new, d)

    def tok_minor(a):
        return jnp.transpose(a, (0, 1, 3, 2))

    def seq_minor(a):
        return jnp.transpose(a.reshape(a.shape[0], t_new, bd), (2, 1, 0))
    return (y_prompt, y_sample,
            ckv_p, tok_minor(kpet_p),
            tok_minor(fkt_p).reshape(1, b, l, HEADS, FOX_DIM), tok_minor(fvt_p).reshape(1, b, l, HEADS, FOX_DIM),
            tok_minor(lft_p),
            ckv_s, seq_minor(kpet_s)[None],
            seq_minor(fkt_s).reshape(1, bd, t_new, HEADS, FOX_DIM),
            seq_minor(fvt_s).reshape(1, bd, t_new, HEADS, FOX_DIM),
            seq_minor(lft_s)[None])
```

```python
import functools

import jax
import jax.numpy as jnp
from jax import lax
from jax.experimental import pallas as pl
from jax.experimental.pallas import tpu as pltpu

F32 = jnp.float32
BF16 = jnp.bfloat16

D_MODEL = 1024
N_META = 16
HEADS = 8
MLA_NOPE = 64
MLA_ROPE = 32
Q_RANK = 256
KV_RANK = 128
FOX_DIM = 64
D_FOX = HEADS * FOX_DIM
D_MLA = HEADS * 64
ROPE_THETA = 10000.0
EPS = 1e-6
NEG = -1e30
LOG2E = 1.4426950408889634
MLA_SCALE = (MLA_NOPE + MLA_ROPE) ** -0.5 * LOG2E
FOX_SCALE = FOX_DIM ** -0.5 * LOG2E
SPLITS = (Q_RANK, KV_RANK, MLA_ROPE, D_MLA, D_FOX, D_FOX, D_FOX, HEADS, D_FOX)

LANES = 128
C_CQ, C_CKV, C_KPE, C_KPESW, C_ZM, C_FQ, C_ZF, C_FL, C_FK, C_FV, C_END = (
    0, 256, 384, 512, 640, 1152, 1664, 2176, 2304, 2816, 3328)

PROJ_TM = 256
ROWS_TM = 128
ATT_TQ = 256
MERGE_TM = 512
SUF_PAGES = 64
DEC_PAGES = 16
VMEM_LIMIT = 56 * 1024 * 1024


def _nt(a, b):
    return lax.dot_general(a, b, (((1,), (1,)), ((), ())), preferred_element_type=F32)


def _mm(a, b):
    return jnp.dot(a, b, preferred_element_type=F32)


def _rms(x, g):
    return x * lax.rsqrt(jnp.mean(x * x, axis=-1, keepdims=True) + EPS) * g


def _split3(x):
    hi = x.astype(BF16)
    r1 = x - hi.astype(F32)
    mid = r1.astype(BF16)
    lo = (r1 - mid.astype(F32)).astype(BF16)
    return hi, mid, lo


def _mm_exact(x, u):
    hi, mid, lo = _split3(x)
    return _mm(hi, u) + _mm(mid, u) + _mm(lo, u)


def _log_sigmoid(x):
    return jnp.minimum(x, 0.0) - jnp.log(1.0 + jnp.exp(-jnp.abs(x)))


def _silu(x):
    return x / (1.0 + jnp.exp(-x))


def _proj_math(x, cos, sin, g_pre, w_a, g_q, w_uq, w_uk, g_kv, b_f):
    h = _rms(x, g_pre).astype(BF16)
    p = _mm(h, w_a)
    cqn = _rms(p[:, C_CQ:C_CKV], g_q).astype(BF16)
    qall = _mm(cqn, w_uq)
    cos2 = jnp.concatenate([cos, cos], axis=1)
    sin2 = jnp.concatenate([sin, sin], axis=1)
    q_pe = (qall[:, 512:768] * cos2 + qall[:, 768:1024] * sin2) * MLA_SCALE
    q_lat = _mm(qall[:, 0:512].astype(BF16), w_uk) * MLA_SCALE
    ckv = _rms(p[:, C_CKV:C_KPE], g_kv)
    kr4 = p[:, C_KPE:C_KPESW] * cos + p[:, C_KPESW:C_ZM] * sin
    zs = jnp.concatenate([_silu(p[:, C_ZM:C_FQ]), _silu(p[:, C_ZF:C_FL])], axis=1)
    fq = p[:, C_FQ:C_ZF] * FOX_SCALE
    lf = _log_sigmoid(p[:, C_FL:C_FK] + b_f)
    return h, p, dict(q_lat=q_lat, q_pe=q_pe, ckv=ckv, kr4=kr4, zs=zs, fq=fq, lf=lf)


def _proj_rows_kernel(running, x_ref, cos_ref, sin_ref, gpre_ref, win_ref, gq_ref, wuq_ref, wuk_ref,
                      gkv_ref, bf_ref, wt_ref, u_ref,
                      ckv_o, kpet_o, fkt_o, fvt_o, lft_o, ft_o, kc_o, fkt16_o, fvt16_o,
                      qlat_o, qpe_o, fq_o, zs_o, kpe_o, fk16_o, fv16_o, carry_ref):
    h, p, r = _proj_math(x_ref[...], cos_ref[...], sin_ref[...], gpre_ref[...], win_ref[...], gq_ref[...],
                         wuq_ref[...], wuk_ref[...], gkv_ref[...], bf_ref[...])
    pt = _nt(wt_ref[...], h)
    ckv_o[...] = r["ckv"]
    kpet_o[...] = r["kr4"].T[0:MLA_ROPE, :]
    fkt_o[...] = pt[0:D_FOX]
    fvt_o[...] = pt[D_FOX:2 * D_FOX]
    lft = r["lf"].T[0:HEADS, :]
    lft_o[...] = lft
    if running:
        @pl.when(pl.program_id(0) == 0)
        def _():
            carry_ref[...] = jnp.zeros(carry_ref.shape, F32)
        ft = carry_ref[...] + lft
        carry_ref[...] = ft
    else:
        ft = _mm_exact(lft, u_ref[...])
    ft_o[...] = ft
    kc_o[...] = jnp.concatenate([r["ckv"], r["kr4"]], axis=1).astype(BF16)
    fkt16_o[...] = pt[0:D_FOX].astype(BF16)
    fvt16_o[...] = pt[D_FOX:2 * D_FOX].astype(BF16)
    qlat_o[...] = r["q_lat"].astype(BF16)
    qpe_o[...] = r["q_pe"].astype(BF16)
    fq_o[...] = r["fq"].astype(BF16)
    zs_o[...] = r["zs"].astype(BF16)
    kpe_o[...] = r["kr4"][:, 0:MLA_ROPE].astype(BF16)
    fk16_o[...] = p[:, C_FK:C_FV].astype(BF16)
    fv16_o[...] = p[:, C_FV:C_END].astype(BF16)


def _proj_prompt_kernel(x_ref, cos_ref, sin_ref, mckv_ref, mkpet_ref, mfkt_ref, mfvt_ref, mlft_ref, mft_ref,
                        gpre_ref, win_ref, gq_ref, wuq_ref, wuk_ref, gkv_ref, bf_ref, wt_ref, u_ref,
                        ckv_o, kpet_o, fkt_o, fvt_o, lft_o,
                        q_o, kc_o, fq_o, fkt16_o, fvt16_o, zs_o, ft_o,
                        carry_ref, tail_kpe, tail_fk, tail_fv, tail_lf):
    blk = pl.program_id(1)
    tm = PROJ_TM
    n_sub = x_ref.shape[1] // tm
    n_c = pl.num_programs(1) * n_sub

    @pl.when(blk == 0)
    def _():
        ckv_o[0, 0, 0:N_META, :] = mckv_ref[0:N_META, :]
        tail_kpe[...] = mkpet_ref[...]
        tail_fk[...] = mfkt_ref[...]
        tail_fv[...] = mfvt_ref[...]
        tail_lf[...] = mlft_ref[...]
        carry_ref[...] = jnp.broadcast_to(mft_ref[:, N_META - 1:N_META], (HEADS, LANES))

    first = lax.broadcasted_iota(jnp.int32, (1, LANES), 1) < N_META
    group = lax.broadcasted_iota(jnp.int32, (1, LANES), 1) // MLA_ROPE

    def chunk(c, rows):
        h, _, r = _proj_math(x_ref[0, rows, :], cos_ref[rows, :], sin_ref[rows, :], gpre_ref[...],
                             win_ref[:, 0:C_FK], gq_ref[...], wuq_ref[...], wuk_ref[...], gkv_ref[...],
                             bf_ref[...])
        pt = _nt(wt_ref[...], h)
        lft = r["lf"].T[0:HEADS, :]

        off = pl.multiple_of(N_META + c * tm, N_META)
        ckv_o[0, 0, pl.ds(off, tm), :] = r["ckv"]

        def shifted_store(out_ref, tail_ref, cur):
            prev = tail_ref[...]
            for k in range(tm // LANES):
                rk = pltpu.roll(cur[:, LANES * k:LANES * (k + 1)], N_META, axis=1)
                col = pl.multiple_of(c * tm + LANES * k, LANES)
                out_ref[0, 0, :, pl.ds(col, LANES)] = jnp.where(first, prev, rk)
                prev = rk
            tail_ref[...] = prev

            @pl.when(c == n_c - 1)
            def _():
                end = out_ref.shape[3] - N_META
                out_ref[0, 0, :, end:end + N_META] = prev[:, 0:N_META]

        shifted_store(kpet_o, tail_kpe, r["kr4"].T[0:MLA_ROPE, :])
        shifted_store(fkt_o, tail_fk, pt[0:D_FOX])
        shifted_store(fvt_o, tail_fv, pt[D_FOX:2 * D_FOX])
        shifted_store(lft_o, tail_lf, lft)

        kc_o[0, rows, :] = jnp.concatenate([r["ckv"], r["kr4"]], axis=1).astype(BF16)
        fkt16_o[0, :, rows] = pt[0:D_FOX].astype(BF16)
        fvt16_o[0, :, rows] = pt[D_FOX:2 * D_FOX].astype(BF16)
        fq_o[0, rows, :] = r["fq"].astype(BF16)
        zs_o[0, rows, :] = r["zs"].astype(BF16)
        for hh in range(HEADS):
            half = r["q_pe"][:, LANES * (hh // 4):LANES * (hh // 4 + 1)]
            pe = jnp.where(group == hh % 4, half, 0.0)
            q_o[0, hh, rows, :] = jnp.concatenate([r["q_lat"][:, LANES * hh:LANES * (hh + 1)], pe],
                                                  axis=1).astype(BF16)
        ft = carry_ref[:, 0:1] + _mm_exact(lft, u_ref[...])
        ft_o[0, :, rows] = ft
        carry_ref[...] = jnp.broadcast_to(ft[:, tm - 1:tm], (HEADS, LANES))

    for k in range(n_sub):
        chunk(blk * n_sub + k, slice(k * tm, (k + 1) * tm))


def _attn_prompt_kernel(q_ref, fq_ref, kc_ref, fkt_ref, fvt_ref, ft_ref, mkc_ref, mfkt_ref, mfvt_ref, mft_ref,
                        olat_o, of_o, mm_s, am_s, *fox_s):
    mf_s, af_s = fox_s[0:HEADS // 2], fox_s[HEADS // 2:]
    tq = fq_ref.shape[1]
    qi = pl.program_id(1)
    q = q_ref[0].reshape(HEADS * tq, 2 * LANES)
    fq = fq_ref[0]
    lo = lax.broadcasted_iota(jnp.int32, (1, LANES), 1) < FOX_DIM
    zero = jnp.zeros((), BF16)
    qp = []
    for p in range(HEADS // 2):
        blk = fq[:, LANES * p:LANES * (p + 1)]
        qp.append(jnp.concatenate([jnp.where(lo, blk, zero), jnp.where(lo, zero, blk)], axis=0))

    for m_ref in (mm_s,) + tuple(mf_s):
        m_ref[...] = jnp.full(m_ref.shape, NEG, F32)
    for a_ref in (am_s,) + tuple(af_s):
        a_ref[...] = jnp.zeros(a_ref.shape, F32)

    def online(s, pv, m_ref, a_ref):
        tk = s.shape[1]
        m_old = m_ref[...]
        m_new = jnp.maximum(m_old, jnp.max(s, axis=1, keepdims=True))
        alpha = jnp.exp2(m_old - m_new)
        p = jnp.exp2(s - jnp.concatenate([m_new] * (tk // LANES), axis=1))
        a_ref[...] = jnp.concatenate([alpha, alpha], axis=1) * a_ref[...] + pv(p.astype(BF16))
        m_ref[...] = m_new

    def update(kc, fkt, fvt, ft, mask):
        tk = kc.shape[0]
        ft = ft * LOG2E
        s = _nt(q, kc)
        if mask is not None:
            s = jnp.where(mask[None], s.reshape(HEADS, tq, tk), NEG).reshape(HEADS * tq, tk)
        v1 = jnp.concatenate([kc[:, 0:KV_RANK], jnp.ones((tk, LANES), BF16)], axis=1)
        online(s, lambda pr: _mm(pr, v1), mm_s, am_s)
        for p in range(HEADS // 2):
            s = _mm(qp[p], fkt[LANES * p:LANES * (p + 1), :]).reshape(2, tq, tk)
            s = s - ft[2 * p:2 * p + 2, :][:, None, :]
            if mask is not None:
                s = jnp.where(mask[None], s, NEG)
            v1t = jnp.concatenate([fvt[LANES * p:LANES * (p + 1), :], jnp.ones((LANES, tk), BF16)], axis=0)
            online(s.reshape(2 * tq, tk), lambda pr, v1t=v1t: _nt(pr, v1t), mf_s[p], af_s[p])

    meta_mask = lax.broadcasted_iota(jnp.int32, (tq, LANES), 1) < N_META
    update(mkc_ref[...], mfkt_ref[...], mfvt_ref[...], mft_ref[...], meta_mask)

    def full_block(kb):
        off = pl.multiple_of(kb * tq, tq)
        update(kc_ref[0, pl.ds(off, tq), :], fkt_ref[0, :, pl.ds(off, tq)], fvt_ref[0, :, pl.ds(off, tq)],
               ft_ref[0, :, pl.ds(off, tq)], None)

    odd = qi % 2

    @pl.when(odd == 1)
    def _():
        full_block(0)

    def body(i, carry):
        full_block(odd + 2 * i)
        full_block(odd + 2 * i + 1)
        return carry

    lax.fori_loop(0, qi // 2, body, 0)
    off = pl.multiple_of(qi * tq, tq)
    causal = (lax.broadcasted_iota(jnp.int32, (tq, tq), 1) <= lax.broadcasted_iota(jnp.int32, (tq, tq), 0))
    update(kc_ref[0, pl.ds(off, tq), :], fkt_ref[0, :, pl.ds(off, tq)], fvt_ref[0, :, pl.ds(off, tq)],
           ft_ref[0, :, pl.ds(off, tq)], causal)

    a = am_s[...]
    o = a[:, 0:LANES] / a[:, LANES:2 * LANES]
    olat_o[0] = jnp.concatenate([o[hh * tq:(hh + 1) * tq] for hh in range(HEADS)], axis=1).astype(BF16)
    for p in range(HEADS // 2):
        a = af_s[p][...]
        o = a[:, 0:LANES] / a[:, LANES:2 * LANES]
        of_o[0, :, LANES * p:LANES * (p + 1)] = jnp.where(lo, o[0:tq], o[tq:2 * tq]).astype(BF16)


def _paged_fetch(copies_for_page, n_pages):
    b, j = pl.program_id(0), pl.program_id(1)
    n_b, n_j = pl.num_programs(0), pl.num_programs(1)
    step = b * n_j + j
    slot = step % 2

    def for_pages(bb, jj, sl, start):
        def body(i, carry):
            for cp in copies_for_page(bb, jj, sl, i):
                if start:
                    cp.start()
                else:
                    cp.wait()
            return carry
        lax.fori_loop(0, n_pages, body, 0)

    @pl.when(step == 0)
    def _():
        for_pages(b, j, slot, True)

    @pl.when(step + 1 < n_b * n_j)
    def _():
        wrap = j + 1 == n_j
        for_pages(jnp.where(wrap, b + 1, b), jnp.where(wrap, 0, j + 1), 1 - slot, True)

    return slot, lambda: for_pages(b, j, slot, False)


def _suffix_kernel(pt_ref, u_ref, lft_hbm, out_ref, buf, sem, carry_ref):
    n = SUF_PAGES
    n_pages = pt_ref.shape[1]
    j = pl.program_id(1)

    def copies(bb, jj, sl, i):
        pid = pt_ref[bb, n_pages - (jj + 1) * n + i]
        rows = pl.ds(pl.multiple_of(i * HEADS, HEADS), HEADS)
        return [pltpu.make_async_copy(lft_hbm.at[0, pid], buf.at[sl, rows, :], sem.at[sl])]

    slot, wait = _paged_fetch(copies, n)

    @pl.when(j == 0)
    def _():
        carry_ref[...] = jnp.zeros(carry_ref.shape, F32)

    wait()
    hi, mid, lo = _split3(buf[slot])
    w_all = _mm(hi, u_ref[...]) + _mm(mid, u_ref[...]) + _mm(lo, u_ref[...])
    cur = carry_ref[...]
    for i in reversed(range(n)):
        w = w_all[HEADS * i:HEADS * (i + 1)]
        out_ref[0, :, LANES * i:LANES * (i + 1)] = w[:, 0:LANES] + cur
        cur = cur + w[:, LANES:2 * LANES]
    carry_ref[...] = cur


def _decode_kernel(pt_ref, suf_ref, qlat_ref, qpe_ref, fq_ref, nckv_ref, nkpe_ref, nfk_ref, nfv_ref, gt_ref,
                   ckv_hbm, kpet_hbm, fkt_hbm, fvt_hbm, olat_o, of_o,
                   ckv_buf, kpet_buf, fkt_buf, fvt_buf, sems, mm_s, lm_s, am_s, mf_s, lf_s, af_s, qbd_s):
    n = DEC_PAGES
    page = ckv_hbm.shape[2]
    j = pl.program_id(1)
    rows = qlat_ref.shape[1]
    t_new = rows // HEADS

    def copies(bb, jj, sl, i):
        pid = pt_ref[bb, jj * n + i]
        keys = pl.ds(pl.multiple_of(i * page, page), page)
        return [pltpu.make_async_copy(ckv_hbm.at[0, pid], ckv_buf.at[sl, keys, :], sems.at[sl, 0]),
                pltpu.make_async_copy(kpet_hbm.at[0, pid], kpet_buf.at[sl, :, keys], sems.at[sl, 1]),
                pltpu.make_async_copy(fkt_hbm.at[0, pid], fkt_buf.at[sl, :, keys], sems.at[sl, 2]),
                pltpu.make_async_copy(fvt_hbm.at[0, pid], fvt_buf.at[sl, :, keys], sems.at[sl, 3])]

    slot, wait = _paged_fetch(copies, n)

    def online(s, pv, m_ref, l_ref, a_ref):
        m_old = m_ref[...]
        m_new = jnp.maximum(m_old, jnp.max(s, axis=1, keepdims=True))
        alpha = jnp.exp2(m_old - m_new)
        p = jnp.exp2(s - jnp.concatenate([m_new] * (s.shape[1] // LANES), axis=1))
        l_ref[...] = alpha * l_ref[...] + jnp.sum(p, axis=1, keepdims=True)
        a_ref[...] = jnp.concatenate([alpha] * (a_ref.shape[1] // LANES), axis=1) * a_ref[...] + pv(p.astype(BF16))
        m_ref[...] = m_new

    @pl.when(j == 0)
    def _():
        mm_s[...] = jnp.full(mm_s.shape, NEG, F32)
        lm_s[...] = jnp.zeros(lm_s.shape, F32)
        am_s[...] = jnp.zeros(am_s.shape, F32)
        mf_s[...] = jnp.full(mf_s.shape, NEG, F32)
        lf_s[...] = jnp.zeros(lf_s.shape, F32)
        af_s[...] = jnp.zeros(af_s.shape, F32)
        fq = fq_ref[0].astype(F32)
        rep = jnp.concatenate([jnp.broadcast_to(fq[t:t + 1, :], (HEADS, D_FOX)) for t in range(t_new)], axis=0)
        row_head = lax.broadcasted_iota(jnp.int32, (rows, D_FOX), 0) % HEADS
        lane_head = lax.broadcasted_iota(jnp.int32, (rows, D_FOX), 1) // FOX_DIM
        qbd_s[...] = jnp.where(row_head == lane_head, rep, 0.0).astype(BF16)

        def pad(x):
            x = x.astype(F32)
            return jnp.concatenate([x, jnp.zeros((LANES - t_new, x.shape[1]), F32)], axis=0).astype(BF16)
        ckv, kpe, fk, fv = pad(nckv_ref[0, 0]), pad(nkpe_ref[0]), pad(nfk_ref[0]), pad(nfv_ref[0])
        key = lax.broadcasted_iota(jnp.int32, (rows, LANES), 1)
        tok = lax.broadcasted_iota(jnp.int32, (rows, LANES), 0) // HEADS
        mask = key <= tok
        s = jnp.where(mask, _nt(qlat_ref[0], ckv) + _nt(qpe_ref[0], kpe), NEG)
        online(s, lambda p: _mm(p, ckv), mm_s, lm_s, am_s)
        bias = jnp.concatenate([gt_ref[0] * LOG2E] * t_new, axis=0)
        s = jnp.where(mask, _nt(qbd_s[...], fk) - bias, NEG)
        online(s, lambda p: _mm(p, fv), mf_s, lf_s, af_s)

    wait()
    ckv = ckv_buf[slot].astype(BF16)
    kpet = kpet_buf[slot].astype(BF16)
    fkt = fkt_buf[slot].astype(BF16)
    fvt = fvt_buf[slot].astype(BF16)
    s = _nt(qlat_ref[0], ckv) + _mm(qpe_ref[0], kpet)
    online(s, lambda p: _mm(p, ckv), mm_s, lm_s, am_s)
    bias = jnp.concatenate([suf_ref[0] * LOG2E] * t_new, axis=0)
    s = _mm(qbd_s[...], fkt) + bias
    online(s, lambda p: _nt(p, fvt), mf_s, lf_s, af_s)

    @pl.when(j == pl.num_programs(1) - 1)
    def _():
        olat_o[0] = (am_s[...] / lm_s[...]).astype(BF16)
        o = af_s[...] / jnp.concatenate([lf_s[...]] * (D_FOX // LANES), axis=1)
        row_head = lax.broadcasted_iota(jnp.int32, (rows, D_FOX), 0) % HEADS
        lane_head = lax.broadcasted_iota(jnp.int32, (rows, D_FOX), 1) // FOX_DIM
        o = jnp.where(row_head == lane_head, o, 0.0)
        of_o[0] = jnp.concatenate(
            [jnp.sum(o[HEADS * t:HEADS * (t + 1)], axis=0, keepdims=True) for t in range(t_new)],
            axis=0).astype(BF16)


def _merge_kernel(olat_ref, of_ref, zs_ref, x_ref, wuv_ref, gom_ref, gof_ref, wo_ref, gpost_ref, y_o):
    o_m = _mm(olat_ref[...], wuv_ref[...])
    zs = zs_ref[...].astype(F32)
    y = jnp.concatenate([_rms(o_m, gom_ref[...]) * zs[:, 0:D_MLA],
                         _rms(of_ref[...].astype(F32), gof_ref[...]) * zs[:, D_MLA:]], axis=1)
    out = _mm(y.astype(BF16), wo_ref[...])
    y_o[...] = x_ref[...] + _rms(out, gpost_ref[...])


def _params(*sem):
    return pltpu.CompilerParams(dimension_semantics=sem, vmem_limit_bytes=VMEM_LIMIT)


def _full(shape):
    return pl.BlockSpec(shape, lambda *_: (0,) * len(shape))


def _prep_weights(w_in, w_uq, w_uk, w_uv):
    o = [0]
    for s in SPLITS:
        o.append(o[-1] + s)
    cq, ckv, kpe, zm, fq, fk, fv, fl, zf = [w_in[:, o[i]:o[i + 1]] for i in range(len(SPLITS))]
    half = MLA_ROPE // 2
    kpe_sw = jnp.concatenate([-kpe[:, half:], kpe[:, :half]], axis=1)
    w_in_p = jnp.concatenate([cq, ckv, jnp.tile(kpe, (1, 4)), jnp.tile(kpe_sw, (1, 4)), zm, fq, zf,
                              jnp.pad(fl, ((0, 0), (0, LANES - HEADS))), fk, fv], axis=1).astype(BF16)
    w_t = jnp.concatenate([fk, fv], axis=1).T.astype(BF16)
    wq = w_uq.reshape(Q_RANK, HEADS, MLA_NOPE + MLA_ROPE)
    pe = wq[:, :, MLA_NOPE:]
    pe_sw = jnp.concatenate([-pe[:, :, half:], pe[:, :, :half]], axis=2)
    w_uq_p = jnp.concatenate([wq[:, :, :MLA_NOPE].reshape(Q_RANK, -1), pe.reshape(Q_RANK, -1),
                              pe_sw.reshape(Q_RANK, -1)], axis=1).astype(BF16)
    eye = jnp.eye(HEADS, dtype=w_uk.dtype)
    w_uk_bd = jnp.einsum('rhd,hg->hdgr', w_uk, eye).reshape(HEADS * MLA_NOPE, HEADS * KV_RANK).astype(BF16)
    w_uv_bd = jnp.einsum('rhd,hg->hrgd', w_uv, eye).reshape(HEADS * KV_RANK, D_MLA).astype(BF16)
    return w_in_p, w_t, w_uq_p, w_uk_bd, w_uv_bd


def _rope_tables(pos):
    half = MLA_ROPE // 2
    inv = ROPE_THETA ** (-jnp.arange(half, dtype=F32) / half)
    ang = pos.astype(F32)[:, None] * inv[None, :]
    return jnp.tile(jnp.cos(ang), (1, LANES // half)), jnp.tile(jnp.sin(ang), (1, LANES // half))


def _proj_rows(x, cos, sin, u, wts, running):
    n = x.shape[0]
    tm = ROWS_TM
    row = lambda w: pl.BlockSpec((tm, w), lambda i: (i, 0))
    col = lambda h: pl.BlockSpec((h, tm), lambda i: (0, i))
    outs = ([(row(KV_RANK), (n, KV_RANK), F32)]
            + [(col(h), (h, n), F32) for h in (MLA_ROPE, D_FOX, D_FOX, HEADS, HEADS)]
            + [(row(2 * LANES), (n, 2 * LANES), BF16), (col(D_FOX), (D_FOX, n), BF16),
               (col(D_FOX), (D_FOX, n), BF16)]
            + [(row(w), (n, w), BF16) for w in (HEADS * KV_RANK, HEADS * MLA_ROPE, D_FOX, 2 * D_FOX,
                                                MLA_ROPE, D_FOX, D_FOX)])
    return pl.pallas_call(
        functools.partial(_proj_rows_kernel, running),
        grid=(n // tm,),
        in_specs=[row(D_MODEL), row(LANES), row(LANES)] + [_full(w.shape) for w in wts] + [_full(u.shape)],
        out_specs=[o[0] for o in outs],
        out_shape=[jax.ShapeDtypeStruct(o[1], o[2]) for o in outs],
        scratch_shapes=[pltpu.VMEM((HEADS, tm), F32)],
        compiler_params=_params("arbitrary"),
        name="proj_rows",
    )(x, cos, sin, *wts, u)


def _proj_prompt(x, cos, sin, meta, u, wts):
    b, s, _ = x.shape
    tm = PROJ_TM
    l = s + N_META
    chunk = lambda w: pl.BlockSpec((1, tm, w), lambda i, c: (i, c, 0))
    chunk_t = lambda h: pl.BlockSpec((1, h, tm), lambda i, c: (i, 0, c))
    tab = pl.BlockSpec((tm, LANES), lambda i, c: (c, 0))
    res_t = lambda h: pl.BlockSpec((1, 1, h, l), lambda i, c: (0, i, 0, 0))
    t_rows = [MLA_ROPE, D_FOX, D_FOX, HEADS]
    return pl.pallas_call(
        _proj_prompt_kernel,
        grid=(b, s // tm),
        in_specs=[chunk(D_MODEL), tab, tab] + [_full(m.shape) for m in meta]
        + [_full(w.shape) for w in wts] + [_full(u.shape)],
        out_specs=[pl.BlockSpec((1, 1, l, KV_RANK), lambda i, c: (0, i, 0, 0))] + [res_t(h) for h in t_rows]
        + [pl.BlockSpec((1, HEADS, tm, 2 * LANES), lambda i, c: (i, 0, c, 0)),
           chunk(2 * LANES), chunk(D_FOX), chunk_t(D_FOX), chunk_t(D_FOX), chunk(2 * D_FOX), chunk_t(HEADS)],
        out_shape=[jax.ShapeDtypeStruct((1, b, l, KV_RANK), F32)]
        + [jax.ShapeDtypeStruct((1, b, h, l), F32) for h in t_rows]
        + [jax.ShapeDtypeStruct((b, HEADS, s, 2 * LANES), BF16),
           jax.ShapeDtypeStruct((b, s, 2 * LANES), BF16), jax.ShapeDtypeStruct((b, s, D_FOX), BF16),
           jax.ShapeDtypeStruct((b, D_FOX, s), BF16), jax.ShapeDtypeStruct((b, D_FOX, s), BF16),
           jax.ShapeDtypeStruct((b, s, 2 * D_FOX), BF16), jax.ShapeDtypeStruct((b, HEADS, s), F32)],
        scratch_shapes=[pltpu.VMEM((HEADS, LANES), F32), pltpu.VMEM((MLA_ROPE, LANES), F32),
                        pltpu.VMEM((D_FOX, LANES), F32), pltpu.VMEM((D_FOX, LANES), F32),
                        pltpu.VMEM((HEADS, LANES), F32)],
        compiler_params=_params("arbitrary", "arbitrary"),
        name="proj_prompt",
    )(x, cos, sin, *meta, *wts, u)


def _attn_prompt(q, fq, kc, fkt16, fvt16, ft, mkc, mfkt16, mfvt16, mft):
    b, s, _ = fq.shape
    tq = ATT_TQ
    qblk = lambda w: pl.BlockSpec((1, tq, w), lambda i, j: (i, j, 0))
    seq_t = lambda h: pl.BlockSpec((1, h, s), lambda i, j: (i, 0, 0))
    return pl.pallas_call(
        _attn_prompt_kernel,
        grid=(b, s // tq),
        in_specs=[pl.BlockSpec((1, HEADS, tq, 2 * LANES), lambda i, j: (i, 0, j, 0)), qblk(D_FOX),
                  pl.BlockSpec((1, s, 2 * LANES), lambda i, j: (i, 0, 0)), seq_t(D_FOX), seq_t(D_FOX),
                  seq_t(HEADS),
                  _full(mkc.shape), _full(mfkt16.shape), _full(mfvt16.shape), _full(mft.shape)],
        out_specs=[qblk(HEADS * KV_RANK), qblk(D_FOX)],
        out_shape=[jax.ShapeDtypeStruct((b, s, HEADS * KV_RANK), BF16),
                   jax.ShapeDtypeStruct((b, s, D_FOX), BF16)],
        scratch_shapes=[pltpu.VMEM((HEADS * tq, LANES), F32), pltpu.VMEM((HEADS * tq, 2 * LANES), F32)]
        + [pltpu.VMEM((2 * tq, LANES), F32)] * (HEADS // 2)
        + [pltpu.VMEM((2 * tq, 2 * LANES), F32)] * (HEADS // 2),
        compiler_params=_params("arbitrary", "arbitrary"),
        name="attn_prompt",
    )(q, fq, kc, fkt16, fvt16, ft, mkc, mfkt16, mfvt16, mft)


def _suffix(page_table, c_lft, u):
    bd, n_pages = page_table.shape
    n = SUF_PAGES
    steps = n_pages // n
    page = c_lft.shape[3]
    return pl.pallas_call(
        _suffix_kernel,
        grid_spec=pltpu.PrefetchScalarGridSpec(
            num_scalar_prefetch=1, grid=(bd, steps),
            in_specs=[pl.BlockSpec(u.shape, lambda b, j, pt: (0, 0)), pl.BlockSpec(memory_space=pl.ANY)],
            out_specs=pl.BlockSpec((1, HEADS, n * page), lambda b, j, pt: (b, 0, steps - 1 - j)),
            scratch_shapes=[pltpu.VMEM((2, n * HEADS, page), F32), pltpu.SemaphoreType.DMA((2,)),
                            pltpu.VMEM((HEADS, LANES), F32)]),
        out_shape=jax.ShapeDtypeStruct((bd, HEADS, n_pages * page), F32),
        compiler_params=_params("arbitrary", "arbitrary"),
        name="suffix",
    )(page_table, u, c_lft)


def _decode(page_table, c_ckv, c_kpet, c_fkt, c_fvt, suf, qlat, qpe, fq, nckv, nkpe, nfk, nfv, gt):
    bd, n_pages = page_table.shape
    n = DEC_PAGES
    steps = n_pages // n
    page = c_ckv.shape[2]
    rows = qlat.shape[1]
    t_new = rows // HEADS

    keys = n * page
    per_b = lambda shape: pl.BlockSpec((1,) + shape, lambda b, j, pt: (b,) + (0,) * len(shape))
    hbm = pl.BlockSpec(memory_space=pl.ANY)
    return pl.pallas_call(
        _decode_kernel,
        grid_spec=pltpu.PrefetchScalarGridSpec(
            num_scalar_prefetch=1, grid=(bd, steps),
            in_specs=[pl.BlockSpec((1, HEADS, keys), lambda b, j, pt: (b, 0, j)),
                      per_b((rows, KV_RANK)), per_b((rows, MLA_ROPE)), per_b((t_new, D_FOX)),
                      pl.BlockSpec((1, 1, t_new, KV_RANK), lambda b, j, pt: (0, b, 0, 0)),
                      per_b((t_new, MLA_ROPE)), per_b((t_new, D_FOX)), per_b((t_new, D_FOX)),
                      per_b((HEADS, LANES)), hbm, hbm, hbm, hbm],
            out_specs=[per_b((rows, KV_RANK)), per_b((t_new, D_FOX))],
            scratch_shapes=[pltpu.VMEM((2, keys, KV_RANK), F32), pltpu.VMEM((2, MLA_ROPE, keys), F32),
                            pltpu.VMEM((2, D_FOX, keys), F32), pltpu.VMEM((2, D_FOX, keys), F32),
                            pltpu.SemaphoreType.DMA((2, 4)),
                            pltpu.VMEM((rows, LANES), F32), pltpu.VMEM((rows, LANES), F32),
                            pltpu.VMEM((rows, KV_RANK), F32),
                            pltpu.VMEM((rows, LANES), F32), pltpu.VMEM((rows, LANES), F32),
                            pltpu.VMEM((rows, D_FOX), F32), pltpu.VMEM((rows, D_FOX), BF16)]),
        out_shape=[jax.ShapeDtypeStruct((bd, rows, KV_RANK), BF16),
                   jax.ShapeDtypeStruct((bd, t_new, D_FOX), BF16)],
        compiler_params=_params("arbitrary", "arbitrary"),
        name="decode",
    )(page_table, suf, qlat, qpe, fq, nckv, nkpe, nfk, nfv, gt, c_ckv, c_kpet, c_fkt, c_fvt)


def _merge(olat, of, zs, x, w_uv_bd, g_om, g_of, w_o, g_post, tm):
    n = x.shape[0]
    row = lambda w: pl.BlockSpec((tm, w), lambda i: (i, 0))
    wts = (w_uv_bd, g_om, g_of, w_o, g_post)
    return pl.pallas_call(
        _merge_kernel,
        grid=(n // tm,),
        in_specs=[row(HEADS * KV_RANK), row(D_FOX), row(2 * D_FOX), row(D_MODEL)] + [_full(w.shape) for w in wts],
        out_specs=row(D_MODEL),
        out_shape=jax.ShapeDtypeStruct((n, D_MODEL), F32),
        compiler_params=_params("arbitrary"),
        name="merge",
    )(olat, of, zs, x, *wts)


def kernel(x_prompt, x_sample, cache_ckv, cache_kpe, cache_fox_k, cache_fox_v, cache_fox_logf, page_table,
           meta_tokens, g_pre, g_post, w_in, g_q, w_uq, g_kv, w_uk, w_uv, b_f, g_out_mla, g_out_fox, w_o):
    assert w_in.shape[0] == 1, "single-layer trunk only"
    b, s, d = x_prompt.shape
    bd, t_new, _ = x_sample.shape
    n_pool, page = cache_ckv.shape[1], cache_ckv.shape[2]
    n_pages = page_table.shape[1]
    past_len = n_pages * page
    l = s + N_META

    w_in_p, w_t, w_uq_p, w_uk_bd, w_uv_bd = _prep_weights(w_in[0], w_uq[0], w_uk[0], w_uv[0])
    b_f_row = jnp.pad(b_f[0][None, :], ((0, 0), (0, LANES - HEADS)))
    wts = (g_pre, w_in_p, g_q, w_uq_p, w_uk_bd, g_kv, b_f_row, w_t)
    w_o16 = w_o[0].astype(BF16)

    def tri(n):
        r = jnp.arange(n)
        return (r[:, None] <= r[None, :]).astype(BF16)

    x_meta = jnp.pad(meta_tokens.astype(F32), ((0, ROWS_TM - N_META), (0, 0)))
    cos_m, sin_m = _rope_tables(jnp.arange(ROWS_TM, dtype=jnp.int32))
    (mckv, mkpet, mfkt, mfvt, mlft, mft, mkc, mfkt16, mfvt16) = _proj_rows(
        x_meta, cos_m, sin_m, tri(ROWS_TM), wts, False)[:9]

    cos_p, sin_p = _rope_tables(N_META + jnp.arange(s, dtype=jnp.int32))
    (ckv_p, kpet_p, fkt_p, fvt_p, lft_p, q_p, kc_p, fq_p, fkt16_p, fvt16_p, zs_p, ft_p) = _proj_prompt(
        x_prompt, cos_p, sin_p, (mckv, mkpet, mfkt, mfvt, mlft, mft), tri(PROJ_TM), wts)
    olat_p, of_p = _attn_prompt(q_p, fq_p, kc_p, fkt16_p, fvt16_p, ft_p, mkc, mfkt16, mfvt16, mft)
    y_prompt = _merge(olat_p.reshape(b * s, -1), of_p.reshape(b * s, -1), zs_p.reshape(b * s, -1),
                      x_prompt.reshape(b * s, d), w_uv_bd, g_out_mla, g_out_fox, w_o16, g_post,
                      MERGE_TM).reshape(b, s, d)

    n_s = bd * t_new
    x_s = jnp.transpose(x_sample, (1, 0, 2)).reshape(n_s, d)
    pos_s = past_len + jnp.repeat(jnp.arange(t_new, dtype=jnp.int32), bd)
    cos_s, sin_s = _rope_tables(pos_s)
    (ckv_s, kpet_s, fkt_s, fvt_s, lft_s, gt_s, _, _, _, qlat_s, qpe_s, fq_s, zs_s, kpe16_s, fk16_s,
     fv16_s) = _proj_rows(x_s, cos_s, sin_s, tri(ROWS_TM), wts, True)

    def seq_major(a):
        return jnp.transpose(a.reshape(t_new, bd, a.shape[-1]), (1, 0, 2))
    ckv_s = seq_major(ckv_s)[None]
    gt = jnp.pad(jnp.transpose(gt_s.reshape(HEADS, t_new, bd), (2, 0, 1)),
                 ((0, 0), (0, 0), (0, LANES - t_new)))
    r = jnp.arange(page)
    later = jnp.concatenate([r[:, None] > r[None, :], jnp.ones((page, LANES), bool)], axis=1).astype(BF16)
    suf = _suffix(page_table, jnp.transpose(cache_fox_logf, (0, 1, 3, 2)), later)
    kt = lambda c: jnp.transpose(c, (0, 1, 3, 4, 2)).reshape(1, n_pool, D_FOX, page)
    olat_s, of_s = _decode(
        page_table, cache_ckv, jnp.transpose(cache_kpe, (0, 1, 3, 2)), kt(cache_fox_k), kt(cache_fox_v), suf,
        seq_major(qlat_s).reshape(bd, t_new * HEADS, KV_RANK),
        seq_major(qpe_s).reshape(bd, t_new * HEADS, MLA_ROPE), seq_major(fq_s),
        ckv_s, seq_major(kpe16_s), seq_major(fk16_s), seq_major(fv16_s), gt)
    y_sample = _merge(olat_s.reshape(n_s, -1), of_s.reshape(n_s, -1), seq_major(zs_s).reshape(n_s, -1),
                      x_sample.reshape(n_s, d), w_uv_bd, g_out_mla, g_out_fox, w_o16, g_post,
                      MERGE_TM).reshape(bd, t_new, d)

    def tok_minor(a):
        return jnp.transpose(a, (0, 1, 3, 2))

    def seq_minor(a):
        return jnp.transpose(a.reshape(a.shape[0], t_new, bd), (2, 1, 0))
    return (y_prompt, y_sample,
            ckv_p, tok_minor(kpet_p),
            tok_minor(fkt_p).reshape(1, b, l, HEADS, FOX_DIM), tok_minor(fvt_p).reshape(1, b, l, HEADS, FOX_DIM),
            tok_minor(lft_p),
            ckv_s, seq_minor(kpet_s)[None],
            seq_minor(fkt_s).reshape(1, bd, t_new, HEADS, FOX_DIM),
            seq_minor(fvt_s).reshape(1, bd, t_new, HEADS, FOX_DIM),
            seq_minor(lft_s)[None])
```
